```python
import math
import jax, jax.numpy as jnp
from jax import lax
import numpy as np

D_MODEL = 1024
BATCH = 4
SEQ = 8192
DEPTH = 2

D_FF = 2816
EPS = 1e-6
N_EVEN = (DEPTH + 1) // 2
N_ODD = DEPTH // 2

GLA_HEADS = 4
GLA_VAL = D_MODEL // 2
GLA_DV = GLA_VAL // GLA_HEADS
GLA_DK = GLA_DV // 2
GLA_KEY = GLA_HEADS * GLA_DK
GLA_GATE_RANK = 16
GLA_TAU = 16.0
GLA_CHUNK = 64

POOL_WINDOWS = (2, 4, 8, 16)
POOL_GROUPS = 4
POOL_WIDTH = D_MODEL // 2
POOL_GDIM = POOL_WIDTH // POOL_GROUPS
IN0_WIDTH = 2 * GLA_KEY + 2 * GLA_VAL + GLA_GATE_RANK + POOL_WIDTH

NSA_HDIM = 64
NSA_HEADS = D_MODEL // NSA_HDIM
NSA_KV_GROUPS = 2
NSA_HPG = NSA_HEADS // NSA_KV_GROUPS
NSA_KV = NSA_KV_GROUPS * NSA_HDIM
CMP_BLOCK = 32
CMP_STRIDE = 16
CMP_HIDDEN = 256
SLC_BLOCK = 64
SLC_TOP = 16
WINDOW = 512
Q_BLOCK = 128
IN1_WIDTH = NSA_HEADS * NSA_HDIM + 6 * NSA_KV + 3 * NSA_HEADS
NEG = -1e30
FORCE = 1e4

kernel_name = "hybrid_gla_pool_nsa_macaron"


def rmsnorm(x, g):
    xf = x.astype(jnp.float32)
    y = xf * lax.rsqrt(jnp.mean(xf * xf, axis=-1, keepdims=True) + EPS)
    return (y * g.astype(jnp.float32)).astype(x.dtype)


def swiglu(x, wg, wu, wd):
    return (jax.nn.silu(x @ wg) * (x @ wu)) @ wd


def alibi_slopes(n):
    return np.array([2.0 ** (-8.0 * (i + 1) / n) for i in range(n)], dtype=np.float32)


def gla_chunked(q, k, v, log_a):
    B, T, H, dk = q.shape
    dv = v.shape[-1]
    C = GLA_CHUNK
    N = T // C

    def chunks(z):
        return z.reshape(B, N, C, H, z.shape[-1]).transpose(0, 3, 1, 2, 4)

    q, k, v, log_a = chunks(q * dk ** -0.5), chunks(k), chunks(v), chunks(log_a)
    b = jnp.cumsum(log_a, axis=3)
    b_last = b[:, :, :, -1:, :]
    q_dec = q * jnp.exp(b)
    k_inv = k * jnp.exp(-b)
    k_end = k * jnp.exp(b_last - b)
    causal = jnp.tril(jnp.ones((C, C), dtype=bool))
    att = jnp.where(causal, jnp.einsum('bhnid,bhnjd->bhnij', q_dec, k_inv), 0.0)
    o_intra = jnp.einsum('bhnij,bhnjv->bhniv', att, v)
    d_state = jnp.einsum('bhncd,bhncv->nbhdv', k_end, v)
    chunk_decay = jnp.exp(jnp.moveaxis(b_last[:, :, :, 0, :], 2, 0))

    def step(S, inp):
        dS, dec = inp
        return dec[..., None] * S + dS, S

    _, s_prev = lax.scan(step, jnp.zeros((B, H, dk, dv), q.dtype), (d_state, chunk_decay))
    o_inter = jnp.einsum('bhncd,nbhdv->bhncv', q_dec, s_prev)
    return (o_intra + o_inter).transpose(0, 2, 3, 1, 4).reshape(B, T, H, dv)


def pool_mixer(p, pool_w, pool_scale):
    B, T, _ = p.shape
    pf = p.astype(jnp.float32).reshape(B, T, POOL_GROUPS, POOL_GDIM)
    csum = jnp.cumsum(pf, axis=1)
    tpos = jnp.arange(T)
    groups = []
    for g, w in enumerate(POOL_WINDOWS):
        c = csum[:, :, g]
        c_lag = jnp.pad(c, ((0, 0), (w, 0), (0, 0)))[:, :T]
        cnt = jnp.minimum(tpos + 1, w).astype(jnp.float32)[None, :, None]
        groups.append((c - c_lag) / cnt - pf[:, :, g])
    pooled = jnp.stack(groups, axis=2)
    out = jnp.einsum('btgc,gcd->btgd', pooled, pool_w.astype(jnp.float32))
    return out.reshape(B, T, POOL_WIDTH) * pool_scale.astype(jnp.float32)


def mixer_gla_pool(h, w_in, gate_w2, gate_b, gla_norm, pool_w, pool_scale, w_out):
    B, T, _ = h.shape
    proj = h @ w_in
    cuts = np.cumsum([GLA_KEY, GLA_KEY, GLA_VAL, GLA_VAL, GLA_GATE_RANK]).tolist()
    q, k, v, g, gr, p = jnp.split(proj, cuts, axis=-1)
    f32 = jnp.float32
    log_a = jax.nn.log_sigmoid((gr @ gate_w2 + gate_b).astype(f32)) / GLA_TAU
    o = gla_chunked(q.astype(f32).reshape(B, T, GLA_HEADS, GLA_DK),
                    k.astype(f32).reshape(B, T, GLA_HEADS, GLA_DK),
                    v.astype(f32).reshape(B, T, GLA_HEADS, GLA_DV),
                    log_a.reshape(B, T, GLA_HEADS, GLA_DK))
    o = o * lax.rsqrt(jnp.mean(o * o, axis=-1, keepdims=True) + EPS) * gla_norm.astype(f32)
    o = o * jax.nn.silu(g.astype(f32).reshape(B, T, GLA_HEADS, GLA_DV))
    o_a = o.reshape(B, T, GLA_VAL)
    o_b = pool_mixer(p, pool_w, pool_scale)
    return jnp.concatenate([o_a, o_b], axis=-1).astype(h.dtype) @ w_out


def cmp_to_slc_matrix(ncmp, nslc):
    cs = np.arange(ncmp) * CMP_STRIDE
    ss = np.arange(nslc) * SLC_BLOCK
    ov = np.minimum(cs[:, None] + CMP_BLOCK, ss[None] + SLC_BLOCK) - np.maximum(cs[:, None], ss[None])
    return (np.clip(ov, 0, None) / CMP_BLOCK).astype(np.float32)


def nsa(h, w_in, cmp_pe, cmpk_w1, cmpk_w2, cmpv_w1, cmpv_w2, w_out):
    B, T, _ = h.shape
    G, HPG, dh = NSA_KV_GROUPS, NSA_HPG, NSA_HDIM
    f32 = jnp.float32
    proj = (h @ w_in).astype(f32)
    cuts = (NSA_HEADS * dh + NSA_KV * np.arange(7)).tolist()
    q, kc, vc, ks, vs, kw, vw, gates = jnp.split(proj, cuts, axis=-1)
    q = q.reshape(B, T, G, HPG, dh) * dh ** -0.5
    kc, vc, ks, vs, kw, vw = [z.reshape(B, T, G, dh) for z in (kc, vc, ks, vs, kw, vw)]
    gates = jax.nn.sigmoid(gates.reshape(B, T, G, HPG, 3))

    ncmp = (T - CMP_BLOCK) // CMP_STRIDE + 1
    cidx = np.arange(ncmp)[:, None] * CMP_STRIDE + np.arange(CMP_BLOCK)[None]
    cmp_end = jnp.asarray(cidx[:, -1])

    def compress(z, w1, w2):
        zb = z[:, cidx] + cmp_pe.astype(f32)[None, None, :, None, :]
        zb = zb.transpose(0, 1, 3, 2, 4).reshape(B, ncmp, G, CMP_BLOCK * dh)
        return jax.nn.gelu(zb @ w1) @ w2

    k_cmp = compress(kc, cmpk_w1, cmpk_w2).astype(f32)
    v_cmp = compress(vc, cmpv_w1, cmpv_w2).astype(f32)

    nslc = T // SLC_BLOCK
    n_top = min(SLC_TOP, nslc)
    m_cs = jnp.asarray(cmp_to_slc_matrix(ncmp, nslc))
    k_blk = ks.reshape(B, nslc, SLC_BLOCK, G, dh).transpose(0, 3, 1, 2, 4)
    v_blk = vs.reshape(B, nslc, SLC_BLOCK, G, dh).transpose(0, 3, 1, 2, 4)
    blk_ids = jnp.arange(nslc)
    gather = jax.vmap(jax.vmap(lambda kb, ix: kb[ix]))

    kw_pad = jnp.pad(kw, ((0, 0), (WINDOW, 0), (0, 0), (0, 0)))
    vw_pad = jnp.pad(vw, ((0, 0), (WINDOW, 0), (0, 0), (0, 0)))

    slopes = jnp.asarray(alibi_slopes(NSA_HEADS).reshape(HPG, G).T)

    def block(qb):
        t0 = qb * Q_BLOCK
        qblk = lax.dynamic_slice_in_dim(q, t0, Q_BLOCK, axis=1)
        gblk = lax.dynamic_slice_in_dim(gates, t0, Q_BLOCK, axis=1)
        tpos = t0 + jnp.arange(Q_BLOCK)

        dist_c = tpos[:, None] - cmp_end[None, :]
        valid_c = dist_c >= 0
        s_c = jnp.einsum('bqghd,bngd->bghqn', qblk, k_cmp) - slopes[:, :, None, None] * dist_c
        p_c = jax.nn.softmax(jnp.where(valid_c, s_c, NEG), axis=-1)
        p_c = jnp.where(valid_c.any(-1)[:, None], p_c, 0.0)
        o_c = jnp.einsum('bghqn,bngd->bqghd', p_c, v_cmp)

        imp = jnp.einsum('bghqn,nj->bgqj', p_c, m_cs)
        cur = tpos // SLC_BLOCK
        forced = (blk_ids[None] == 0) | (blk_ids[None] == cur[:, None]) | (blk_ids[None] == cur[:, None] - 1)
        valid_b = (blk_ids[None] * SLC_BLOCK) <= tpos[:, None]
        imp = jnp.where(valid_b, jnp.where(forced, FORCE, imp), -1.0)
        _, sel = lax.top_k(imp, n_top)
        k_sel = gather(k_blk, sel)
        v_sel = gather(v_blk, sel)
        spos = sel[..., None] * SLC_BLOCK + jnp.arange(SLC_BLOCK)
        dist_s = (tpos[None, None, :, None, None] - spos)[:, :, None]
        s_s = jnp.einsum('bqghd,bgqnld->bghqnl', qblk, k_sel) - slopes[None, :, :, None, None, None] * dist_s
        s_s = jnp.where(dist_s >= 0, s_s, NEG)
        sh = s_s.shape
        p_s = jax.nn.softmax(s_s.reshape(sh[:4] + (-1,)), axis=-1).reshape(sh)
        o_s = jnp.einsum('bghqnl,bgqnld->bqghd', p_s, v_sel)

        k_win = lax.dynamic_slice_in_dim(kw_pad, t0, Q_BLOCK + WINDOW, axis=1)
        v_win = lax.dynamic_slice_in_dim(vw_pad, t0, Q_BLOCK + WINDOW, axis=1)
        wpos = t0 - WINDOW + jnp.arange(Q_BLOCK + WINDOW)
        dist_w = tpos[:, None] - wpos[None, :]
        valid_w = (dist_w >= 0) & (dist_w < WINDOW) & (wpos[None, :] >= 0)
        s_w = jnp.einsum('bqghd,bkgd->bghqk', qblk, k_win) - slopes[:, :, None, None] * dist_w
        p_w = jax.nn.softmax(jnp.where(valid_w, s_w, NEG), axis=-1)
        o_w = jnp.einsum('bghqk,bkgd->bqghd', p_w, v_win)

        return gblk[..., 0:1] * o_c + gblk[..., 1:2] * o_s + gblk[..., 2:3] * o_w

    outs = lax.map(block, jnp.arange(T // Q_BLOCK))
    o = outs.transpose(1, 0, 2, 3, 4, 5).reshape(B, T, NSA_HEADS * dh)
    return o.astype(h.dtype) @ w_out


def setup_inputs(seed: int = 0) -> dict:
    key = jax.random.key(seed)
    ks = iter(jax.random.split(key, 40))
    nrm = lambda shape, scale: jax.random.normal(next(ks), shape, jnp.float32) * scale
    gain = lambda shape: 1.0 + nrm(shape, 0.02)
    D, F = D_MODEL, D_FF
    CIN = CMP_BLOCK * NSA_HDIM
    return {
        "x": nrm((BATCH, SEQ, D), 1.0),
        "norm_ffn1": gain((DEPTH, D)),
        "ffn1_wg": nrm((DEPTH, D, F), D ** -0.5),
        "ffn1_wu": nrm((DEPTH, D, F), D ** -0.5),
        "ffn1_wd": nrm((DEPTH, F, D), F ** -0.5),
        "norm_mix": gain((DEPTH, D)),
        "norm_ffn2": gain((DEPTH, D)),
        "ffn2_wg": nrm((DEPTH, D, F), D ** -0.5),
        "ffn2_wu": nrm((DEPTH, D, F), D ** -0.5),
        "ffn2_wd": nrm((DEPTH, F, D), F ** -0.5),
        "a_w_in": nrm((N_EVEN, D, IN0_WIDTH), D ** -0.5),
        "a_gate_w2": nrm((N_EVEN, GLA_GATE_RANK, GLA_KEY), GLA_GATE_RANK ** -0.5),
        "a_gate_b": nrm((N_EVEN, GLA_KEY), 0.1),
        "a_gla_norm": gain((N_EVEN, GLA_DV)),
        "a_pool_w": nrm((N_EVEN, POOL_GROUPS, POOL_GDIM, POOL_GDIM), POOL_GDIM ** -0.5),
        "a_pool_scale": gain((N_EVEN, POOL_WIDTH)),
        "a_w_out": nrm((N_EVEN, D, D), D ** -0.5),
        "c_w_in": nrm((N_ODD, D, IN1_WIDTH), D ** -0.5),
        "c_cmp_pe": nrm((N_ODD, CMP_BLOCK, NSA_HDIM), 0.1),
        "c_cmpk_w1": nrm((N_ODD, CIN, CMP_HIDDEN), CIN ** -0.5),
        "c_cmpk_w2": nrm((N_ODD, CMP_HIDDEN, NSA_HDIM), CMP_HIDDEN ** -0.5),
        "c_cmpv_w1": nrm((N_ODD, CIN, CMP_HIDDEN), CIN ** -0.5),
        "c_cmpv_w2": nrm((N_ODD, CMP_HIDDEN, NSA_HDIM), CMP_HIDDEN ** -0.5),
        "c_w_out": nrm((N_ODD, D, D), D ** -0.5),
        "final_norm": gain((D,)),
    }


def reference(x, norm_ffn1, ffn1_wg, ffn1_wu, ffn1_wd, norm_mix, norm_ffn2, ffn2_wg, ffn2_wu, ffn2_wd,
              a_w_in, a_gate_w2, a_gate_b, a_gla_norm, a_pool_w, a_pool_scale, a_w_out,
              c_w_in, c_cmp_pe, c_cmpk_w1, c_cmpk_w2, c_cmpv_w1, c_cmpv_w2, c_w_out, final_norm):
    h = x
    for l in range(DEPTH):
        h = h + 0.5 * swiglu(rmsnorm(h, norm_ffn1[l]), ffn1_wg[l], ffn1_wu[l], ffn1_wd[l])
        hn = rmsnorm(h, norm_mix[l])
        i = l // 2
        if l % 2 == 0:
            mix = mixer_gla_pool(hn, a_w_in[i], a_gate_w2[i], a_gate_b[i], a_gla_norm[i],
                                 a_pool_w[i], a_pool_scale[i], a_w_out[i])
        else:
            mix = nsa(hn, c_w_in[i], c_cmp_pe[i], c_cmpk_w1[i], c_cmpk_w2[i],
                      c_cmpv_w1[i], c_cmpv_w2[i], c_w_out[i])
        h = h + mix.astype(h.dtype)
        h = h + 0.5 * swiglu(rmsnorm(h, norm_ffn2[l]), ffn2_wg[l], ffn2_wu[l], ffn2_wd[l])
    return rmsnorm(h, final_norm)
```

```python
import functools
import math

import numpy as np
import jax
import jax.numpy as jnp
from jax import lax
from jax.experimental import pallas as pl
from jax.experimental.pallas import tpu as pltpu

F32 = jnp.float32
BF16 = jnp.bfloat16

D_MODEL = 1024
D_FF = 2816
EPS = 1e-6

GLA_HEADS = 4
GLA_DV = 128
GLA_DK = 64
GLA_DK_PAD = 128
GLA_KEY = GLA_HEADS * GLA_DK
GLA_VAL = GLA_HEADS * GLA_DV
GLA_GATE_RANK = 16
GLA_TAU = 16.0
GLA_CHUNK = 64

POOL_WINDOWS = (2, 4, 8, 16)
POOL_GROUPS = 4
POOL_GDIM = 128
POOL_WIDTH = POOL_GROUPS * POOL_GDIM
POOL_HALO = 16

NSA_HDIM = 64
NSA_HEADS = 16
NSA_KV_GROUPS = 2
NSA_HPG = NSA_HEADS // NSA_KV_GROUPS
NSA_PAIRS = NSA_HPG // 2
NSA_KV = NSA_KV_GROUPS * NSA_HDIM
CMP_BLOCK = 32
CMP_STRIDE = 16
CMP_HIDDEN = 256
SLC_BLOCK = 64
SLC_TOP = 16
WINDOW = 512
Q_BLOCK = 128
NEG = -1e30
FORCE = 1e4

LANE = 128
VMEM_LIMIT = 56 * 1024 * 1024


def _cparams(sem):
    return pltpu.CompilerParams(dimension_semantics=sem, vmem_limit_bytes=VMEM_LIMIT)


def _rmsnorm(x, g):
    return x * lax.rsqrt(jnp.mean(x * x, axis=-1, keepdims=True) + EPS) * g


def _dot(a, b):
    return jnp.dot(a, b, preferred_element_type=F32)


def _dot_nt(a, b):
    return lax.dot_general(a, b, (((1,), (1,)), ((), ())), preferred_element_type=F32)


def _split3(x):
    hi = x.astype(BF16)
    r1 = x - hi.astype(F32)
    mid = r1.astype(BF16)
    lo = (r1 - mid.astype(F32)).astype(BF16)
    return hi, mid, lo


def _dot_exact_rhs(x, m_bf16):
    hi, mid, lo = _split3(x)
    return _dot(hi, m_bf16) + _dot(mid, m_bf16) + _dot(lo, m_bf16)


def _ffn_body(h_ref, g_ref, wg_ref, wu_ref, wd_ref, fg_ref, o_ref, xn_ref, acc_ref, *, final_norm):
    j = pl.program_id(1)

    @pl.when(j == 0)
    def _():
        xn_ref[...] = _rmsnorm(h_ref[...], g_ref[...]).astype(BF16)
        acc_ref[...] = jnp.zeros_like(acc_ref)

    xn = xn_ref[...]
    a = _dot(xn, wg_ref[...])
    u = _dot(xn, wu_ref[...])
    hid = (a * jax.nn.sigmoid(a) * u).astype(BF16)
    acc_ref[...] += _dot(hid, wd_ref[...])

    @pl.when(j == pl.num_programs(1) - 1)
    def _():
        y = h_ref[...] + 0.5 * acc_ref[...]
        if final_norm:
            y = _rmsnorm(y, fg_ref[...])
        o_ref[...] = y


def _ffn(h, gain, wg, wu, wd, final_gain, *, final_norm, tm=1024, tf=256):
    m = h.shape[0]
    assert m % tm == 0 and D_FF % tf == 0
    return pl.pallas_call(
        functools.partial(_ffn_body, final_norm=final_norm),
        grid=(m // tm, D_FF // tf),
        in_specs=[
            pl.BlockSpec((tm, D_MODEL), lambda i, j: (i, 0)),
            pl.BlockSpec((1, D_MODEL), lambda i, j: (0, 0)),
            pl.BlockSpec((D_MODEL, tf), lambda i, j: (0, j)),
            pl.BlockSpec((D_MODEL, tf), lambda i, j: (0, j)),
            pl.BlockSpec((tf, D_MODEL), lambda i, j: (j, 0)),
            pl.BlockSpec((1, D_MODEL), lambda i, j: (0, 0)),
        ],
        out_specs=pl.BlockSpec((tm, D_MODEL), lambda i, j: (i, 0)),
        out_shape=jax.ShapeDtypeStruct((m, D_MODEL), F32),
        scratch_shapes=[pltpu.VMEM((tm, D_MODEL), BF16), pltpu.VMEM((tm, D_MODEL), F32)],
        compiler_params=_cparams(("parallel", "arbitrary")),
        name="ffn",
    )(h, gain, wg, wu, wd, final_gain)


W0_Q, W0_K, W0_V, W0_G, W0_P, W0_GR = 0, 512, 1024, 1536, 2048, 2560
W0_COLS = 2688


def _inproj0_body(h_ref, g_ref, w_ref, w2_ref, b2_ref, q_ref, k_ref, v_ref, gg_ref, p_ref, la_ref):
    xn = _rmsnorm(h_ref[...], g_ref[...]).astype(BF16)
    q_ref[...] = _dot(xn, w_ref[:, W0_Q:W0_Q + 512])
    k_ref[...] = _dot(xn, w_ref[:, W0_K:W0_K + 512])
    v_ref[...] = _dot(xn, w_ref[:, W0_V:W0_V + 512])
    gg_ref[...] = _dot(xn, w_ref[:, W0_G:W0_G + 512])
    p_ref[...] = _dot(xn, w_ref[:, W0_P:W0_P + 512])
    gr = _dot(xn, w_ref[:, W0_GR:W0_GR + LANE]).astype(BF16)
    z = _dot(gr, w2_ref[...]) + b2_ref[...]
    log_sig = jnp.minimum(z, 0.0) - jnp.log1p(jnp.exp(-jnp.abs(z)))
    la_ref[...] = log_sig / GLA_TAU


def _inproj0(h, gain, w0, w2, b2, *, tm=512):
    m = h.shape[0]
    wide = pl.BlockSpec((tm, 512), lambda i: (i, 0))
    return pl.pallas_call(
        _inproj0_body,
        grid=(m // tm,),
        in_specs=[
            pl.BlockSpec((tm, D_MODEL), lambda i: (i, 0)),
            pl.BlockSpec((1, D_MODEL), lambda i: (0, 0)),
            pl.BlockSpec((D_MODEL, W0_COLS), lambda i: (0, 0)),
            pl.BlockSpec((LANE, 512), lambda i: (0, 0)),
            pl.BlockSpec((1, 512), lambda i: (0, 0)),
        ],
        out_specs=[wide] * 6,
        out_shape=[jax.ShapeDtypeStruct((m, 512), F32)] * 6,
        compiler_params=_cparams(("parallel",)),
        name="inproj0",
    )(h, gain, w0, w2, b2)


def _gla_body(q_ref, k_ref, v_ref, g_ref, la_ref, gn_ref, o_ref, st_ref, *, tt):
    c = GLA_CHUNK
    nchunk = tt // c

    @pl.when(pl.program_id(2) == 0)
    def _():
        st_ref[...] = jnp.zeros_like(st_ref)

    la = la_ref[0]
    row = lax.broadcasted_iota(jnp.int32, (tt, tt), 0)
    col = lax.broadcasted_iota(jnp.int32, (tt, tt), 1)
    shift = int(math.log2(c))
    same_chunk = (row >> shift) == (col >> shift)
    cum_m = jnp.where(same_chunk & (col <= row), 1.0, 0.0).astype(BF16)
    tot_m = jnp.where(same_chunk, 1.0, 0.0).astype(BF16)
    la3 = _split3(la)
    b = sum(_dot(cum_m, t) for t in la3)
    b_last = sum(_dot(tot_m, t) for t in la3)

    q = q_ref[0] * (GLA_DK ** -0.5)
    k = k_ref[0]
    q_dec = (q * jnp.exp(b)).astype(BF16)
    k_inv = (k * jnp.exp(-b)).astype(BF16)
    k_end = (k * jnp.exp(b_last - b)).astype(BF16)
    decay = jnp.exp(b_last)
    v = v_ref[0]
    v_bf = v.astype(BF16)

    ci = lax.broadcasted_iota(jnp.int32, (c, c), 0)
    cj = lax.broadcasted_iota(jnp.int32, (c, c), 1)
    causal = cj <= ci

    st = st_ref[...]
    outs = []
    for n in range(nchunk):
        sl = slice(n * c, (n + 1) * c)
        att = jnp.where(causal, _dot_nt(q_dec[sl], k_inv[sl]), 0.0)
        o = _dot(att.astype(BF16), v_bf[sl]) + _dot_nt(q_dec[sl], st.astype(BF16))
        outs.append(o)
        d_state = _dot(v[sl].T.astype(BF16), k_end[sl])
        st = decay[n * c:n * c + 1, :] * st + d_state
    st_ref[...] = st

    o = jnp.concatenate(outs, axis=0)
    o = o * lax.rsqrt(jnp.mean(o * o, axis=-1, keepdims=True) + EPS) * gn_ref[...]
    g = g_ref[0]
    o_ref[0] = o * (g * jax.nn.sigmoid(g))


def _gla(q, k, v, g, la, gn, *, tt=512):
    bsz, t, _ = q.shape
    blk = pl.BlockSpec((1, tt, LANE), lambda b, h, i: (b, i, h))
    return pl.pallas_call(
        functools.partial(_gla_body, tt=tt),
        grid=(bsz, GLA_HEADS, t // tt),
        in_specs=[blk, blk, blk, blk, blk, pl.BlockSpec((1, LANE), lambda b, h, i: (0, 0))],
        out_specs=blk,
        out_shape=jax.ShapeDtypeStruct((bsz, t, GLA_VAL), F32),
        scratch_shapes=[pltpu.VMEM((GLA_DV, GLA_DK_PAD), F32)],
        compiler_params=_cparams(("parallel", "parallel", "arbitrary")),
        name="gla",
    )(q, k, v, g, la, gn)


def _outproj0_body(h_ref, oa_ref, p_ref, pw_ref, ps_ref, wo_ref, o_ref, pbuf_ref, *, tm):
    i = pl.program_id(1)

    @pl.when(i == 0)
    def _():
        pbuf_ref[0:POOL_HALO, :] = jnp.zeros((POOL_HALO, POOL_WIDTH), F32)

    p = p_ref[0]
    pbuf_ref[POOL_HALO:POOL_HALO + tm, :] = p
    tpos = i * tm + lax.broadcasted_iota(jnp.int32, (tm, 1), 0)
    mix = _dot(oa_ref[0].astype(BF16), wo_ref[0:GLA_VAL, :])
    for gi, w in enumerate(POOL_WINDOWS):
        ls = slice(gi * POOL_GDIM, (gi + 1) * POOL_GDIM)
        acc = p[:, ls]
        for s in range(1, w):
            acc = acc + pbuf_ref[POOL_HALO - s:POOL_HALO - s + tm, ls]
        cnt = jnp.minimum(tpos + 1, w).astype(F32)
        pooled = acc / cnt - p[:, ls]
        ob = _dot(pooled.astype(BF16), pw_ref[gi]) * ps_ref[:, ls]
        mix = mix + _dot(ob.astype(BF16), wo_ref[GLA_VAL + gi * POOL_GDIM:GLA_VAL + (gi + 1) * POOL_GDIM, :])
    pbuf_ref[0:POOL_HALO, :] = p[tm - POOL_HALO:, :]
    o_ref[0] = h_ref[0] + mix


def _outproj0(h, oa, p, pool_w, pool_scale, w_out, *, tm=512):
    bsz, t, _ = h.shape
    return pl.pallas_call(
        functools.partial(_outproj0_body, tm=tm),
        grid=(bsz, t // tm),
        in_specs=[
            pl.BlockSpec((1, tm, D_MODEL), lambda b, i: (b, i, 0)),
            pl.BlockSpec((1, tm, GLA_VAL), lambda b, i: (b, i, 0)),
            pl.BlockSpec((1, tm, POOL_WIDTH), lambda b, i: (b, i, 0)),
            pl.BlockSpec((POOL_GROUPS, POOL_GDIM, POOL_GDIM), lambda b, i: (0, 0, 0)),
            pl.BlockSpec((1, POOL_WIDTH), lambda b, i: (0, 0)),
            pl.BlockSpec((D_MODEL, D_MODEL), lambda b, i: (0, 0)),
        ],
        out_specs=pl.BlockSpec((1, tm, D_MODEL), lambda b, i: (b, i, 0)),
        out_shape=jax.ShapeDtypeStruct((bsz, t, D_MODEL), F32),
        scratch_shapes=[pltpu.VMEM((POOL_HALO + tm, POOL_WIDTH), F32)],
        compiler_params=_cparams(("parallel", "arbitrary")),
        name="outproj0",
    )(h, oa, p, pool_w, pool_scale, w_out)


KV_DUP = NSA_KV_GROUPS * LANE
W1_Q, W1_KC, W1_VC, W1_KS, W1_VS, W1_KW, W1_VW, W1_GT = 0, 1024, 1152, 1280, 1536, 1792, 2048, 2304
W1_COLS = 2432


def _inproj1_body(h_ref, g_ref, w_ref, q_ref, kc_ref, vc_ref, ks_ref, vs_ref, kw_ref, vw_ref, gt_ref):
    xn = _rmsnorm(h_ref[...], g_ref[...]).astype(BF16)
    q_ref[...] = (_dot(xn, w_ref[:, W1_Q:W1_Q + 1024]) * (NSA_HDIM ** -0.5)).astype(BF16)
    kc_ref[...] = _dot(xn, w_ref[:, W1_KC:W1_KC + NSA_KV])
    vc_ref[...] = _dot(xn, w_ref[:, W1_VC:W1_VC + NSA_KV])
    ks_ref[...] = _dot(xn, w_ref[:, W1_KS:W1_KS + KV_DUP]).astype(BF16)
    vs_ref[...] = _dot(xn, w_ref[:, W1_VS:W1_VS + KV_DUP]).astype(BF16)
    kw_ref[...] = _dot(xn, w_ref[:, W1_KW:W1_KW + KV_DUP]).astype(BF16)
    vw_ref[...] = _dot(xn, w_ref[:, W1_VW:W1_VW + KV_DUP]).astype(BF16)
    gt_ref[...] = jax.nn.sigmoid(_dot(xn, w_ref[:, W1_GT:W1_GT + LANE]))


def _inproj1(h, gain, w1, *, tm=512):
    m = h.shape[0]
    spec = lambda n: pl.BlockSpec((tm, n), lambda i: (i, 0))
    shp = lambda n, dt: jax.ShapeDtypeStruct((m, n), dt)
    return pl.pallas_call(
        _inproj1_body,
        grid=(m // tm,),
        in_specs=[
            pl.BlockSpec((tm, D_MODEL), lambda i: (i, 0)),
            pl.BlockSpec((1, D_MODEL), lambda i: (0, 0)),
            pl.BlockSpec((D_MODEL, W1_COLS), lambda i: (0, 0)),
        ],
        out_specs=[spec(1024), spec(NSA_KV), spec(NSA_KV)] + [spec(KV_DUP)] * 4 + [spec(LANE)],
        out_shape=[shp(1024, BF16), shp(NSA_KV, F32), shp(NSA_KV, F32)] + [shp(KV_DUP, BF16)] * 4 + [shp(LANE, F32)],
        compiler_params=_cparams(("parallel",)),
        name="inproj1",
    )(h, gain, w1)


def _compress_body(z_ref, pea_ref, peb_ref, wa_ref, wb_ref, w2_ref, o_ref, sh_ref, *, nrow):
    z = z_ref[0]
    a = _dot((z + pea_ref[...]).astype(BF16), wa_ref[...])
    bm = _dot((z + peb_ref[...]).astype(BF16), wb_ref[...])
    sh_ref[0:nrow, :] = bm
    sh_ref[nrow:nrow + 8, :] = jnp.zeros((8, sh_ref.shape[1]), F32)
    x = a + sh_ref[1:nrow + 1, :]
    cdf = 0.5 * (1.0 + jnp.tanh(math.sqrt(2.0 / math.pi) * (x + 0.044715 * (x * x * x))))
    hid = (x * cdf).astype(BF16)
    for g in range(NSA_KV_GROUPS):
        o_ref[0, g] = _dot(hid[:, g * CMP_HIDDEN:(g + 1) * CMP_HIDDEN], w2_ref[...]).astype(BF16)


def _compress(z, pea, peb, wa, wb, w2):
    bsz, nrow, zw = z.shape
    hw = NSA_KV_GROUPS * CMP_HIDDEN
    return pl.pallas_call(
        functools.partial(_compress_body, nrow=nrow),
        grid=(bsz,),
        in_specs=[
            pl.BlockSpec((1, nrow, zw), lambda b: (b, 0, 0)),
            pl.BlockSpec((1, zw), lambda b: (0, 0)),
            pl.BlockSpec((1, zw), lambda b: (0, 0)),
            pl.BlockSpec((zw, hw), lambda b: (0, 0)),
            pl.BlockSpec((zw, hw), lambda b: (0, 0)),
            pl.BlockSpec((CMP_HIDDEN, LANE), lambda b: (0, 0)),
        ],
        out_specs=pl.BlockSpec((1, NSA_KV_GROUPS, nrow, LANE), lambda b: (b, 0, 0, 0)),
        out_shape=jax.ShapeDtypeStruct((bsz, NSA_KV_GROUPS, nrow, LANE), BF16),
        scratch_shapes=[pltpu.VMEM((nrow + 8, hw), F32)],
        compiler_params=_cparams(("parallel",)),
        name="compress",
    )(z, pea, peb, wa, wb, w2)


SEL_TK = 512


def _pair_rows(q_ref, pr, lane_lo):
    qp = q_ref[0, :, pr * LANE:(pr + 1) * LANE]
    zero = jnp.zeros_like(qp)
    return jnp.concatenate([jnp.where(lane_lo, qp, zero), jnp.where(lane_lo, zero, qp)], axis=0)


def _nsa_body(slope_ref, q_ref, kc_ref, vc_ref, ks_ref, vs_ref, kw_ref, vw_ref, mcs_ref,
              oc_ref, os_ref, ow_ref, m_scr, l_scr, acc_scr, *, t):
    g = pl.program_id(1)
    qi = pl.program_id(2)
    t0 = qi * Q_BLOCK
    ncmp_pad = kc_ref.shape[2]
    nslc = t // SLC_BLOCK
    nq = Q_BLOCK

    lane_lo = lax.broadcasted_iota(jnp.int32, (nq, LANE), 1) < NSA_HDIM
    tpos = t0 + lax.broadcasted_iota(jnp.int32, (nq, 1), 0)

    def merge(o2):
        return jnp.where(lane_lo, o2[:nq], o2[nq:])

    cmp_end = lax.broadcasted_iota(jnp.int32, (1, ncmp_pad), 1) * CMP_STRIDE + (CMP_BLOCK - 1)
    dist_c = tpos - cmp_end
    valid_c = dist_c >= 0
    dist_cf = dist_c.astype(F32)
    any_c = tpos >= (CMP_BLOCK - 1)
    kc = kc_ref[0, 0]
    vc = vc_ref[0, 0]
    p_sum = jnp.zeros((nq, ncmp_pad), F32)
    for pr in range(NSA_PAIRS):
        s2 = _dot_nt(_pair_rows(q_ref, pr, lane_lo), kc)
        ps = []
        for hh in range(2):
            slope = slope_ref[g * NSA_HPG + 2 * pr + hh]
            s = jnp.where(valid_c, s2[hh * nq:(hh + 1) * nq] - slope * dist_cf, NEG)
            e = jnp.exp(s - jnp.max(s, axis=-1, keepdims=True))
            p = e / jnp.sum(e, axis=-1, keepdims=True)
            p = jnp.where(any_c, p, 0.0)
            p_sum = p_sum + p
            ps.append(p.astype(BF16))
        oc_ref[0, :, pr * LANE:(pr + 1) * LANE] = merge(_dot(jnp.concatenate(ps, axis=0), vc))
    imp = _dot_exact_rhs(p_sum, mcs_ref[...])

    blk = lax.broadcasted_iota(jnp.int32, (nq, nslc), 1)
    slc_shift = int(math.log2(SLC_BLOCK))
    cur = tpos >> slc_shift
    forced = (blk == 0) | (blk == cur) | (blk == cur - 1)
    valid_b = blk * SLC_BLOCK <= tpos
    score = jnp.where(valid_b, jnp.where(forced, FORCE, imp), -1.0)
    blk_f = blk.astype(F32)
    sel = jnp.zeros((nq, nslc), F32)
    for _ in range(min(SLC_TOP, nslc)):
        mx = jnp.max(score, axis=-1, keepdims=True)
        first = jnp.min(jnp.where(score == mx, blk_f, float(nslc)), axis=-1, keepdims=True)
        pick = blk_f == first
        sel = jnp.where(pick, 1.0, sel)
        score = jnp.where(pick, -2.0, score)
    sel_bf = sel.astype(BF16)

    m_scr[...] = jnp.full(m_scr.shape, NEG, F32)
    l_scr[...] = jnp.zeros(l_scr.shape, F32)
    acc_scr[...] = jnp.zeros(acc_scr.shape, F32)
    tk = SEL_TK
    blk_per_tile = tk // SLC_BLOCK

    def sel_step(kt, carry):
        k0 = pl.multiple_of(kt * tk, tk)
        erow = lax.broadcasted_iota(jnp.int32, (nslc, tk), 0)
        ecol = kt * blk_per_tile + (lax.broadcasted_iota(jnp.int32, (nslc, tk), 1) >> slc_shift)
        expand = jnp.where(erow == ecol, 1.0, 0.0).astype(BF16)
        sel_keys = _dot(sel_bf, expand)
        kpos = k0 + lax.broadcasted_iota(jnp.int32, (1, tk), 1)
        rel = kpos - tpos
        bias = jnp.where((sel_keys > 0.5) & (rel <= 0), 0.0, NEG)
        rel_f = rel.astype(F32)
        kt_ = ks_ref[0, pl.ds(k0, tk), :]
        vt_ = vs_ref[0, pl.ds(k0, tk), :]
        for pr in range(NSA_PAIRS):
            s2 = _dot_nt(_pair_rows(q_ref, pr, lane_lo), kt_)
            for hh in range(2):
                hd = 2 * pr + hh
                slope = slope_ref[g * NSA_HPG + hd]
                s = s2[hh * nq:(hh + 1) * nq] + slope * rel_f + bias
                m_old = m_scr[hd]
                m_new = jnp.maximum(m_old, jnp.max(s, axis=-1, keepdims=True))
                alpha = jnp.exp(m_old - m_new)
                p = jnp.exp(s - m_new)
                l_scr[hd] = alpha * l_scr[hd] + jnp.sum(p, axis=-1, keepdims=True)
                acc_scr[hd] = alpha * acc_scr[hd] + _dot(p.astype(BF16), vt_)
                m_scr[hd] = m_new
        return carry

    n_tiles = (t0 + nq + tk - 1) // tk
    lax.fori_loop(0, n_tiles, sel_step, 0)
    for pr in range(NSA_PAIRS):
        o2 = [acc_scr[2 * pr + hh] / l_scr[2 * pr + hh] for hh in range(2)]
        os_ref[0, :, pr * LANE:(pr + 1) * LANE] = jnp.where(lane_lo, o2[0], o2[1])

    wk = WINDOW + nq
    w0 = pl.multiple_of(jnp.maximum(t0 - WINDOW, 0), nq)
    wpos = w0 + lax.broadcasted_iota(jnp.int32, (1, wk), 1)
    dist_w = tpos - wpos
    valid_w = (dist_w >= 0) & (dist_w < WINDOW)
    dist_wf = dist_w.astype(F32)
    kwt = kw_ref[0, pl.ds(w0, wk), :]
    vwt = vw_ref[0, pl.ds(w0, wk), :]
    for pr in range(NSA_PAIRS):
        s2 = _dot_nt(_pair_rows(q_ref, pr, lane_lo), kwt)
        ps = []
        for hh in range(2):
            slope = slope_ref[g * NSA_HPG + 2 * pr + hh]
            s = jnp.where(valid_w, s2[hh * nq:(hh + 1) * nq] - slope * dist_wf, NEG)
            e = jnp.exp(s - jnp.max(s, axis=-1, keepdims=True))
            ps.append((e / jnp.sum(e, axis=-1, keepdims=True)).astype(BF16))
        ow_ref[0, :, pr * LANE:(pr + 1) * LANE] = merge(_dot(jnp.concatenate(ps, axis=0), vwt))


def _nsa_attn(slopes, q, kcmp, vcmp, ks, vs, kw, vw, mcs):
    bsz, t, _ = q.shape
    ncmp_pad = kcmp.shape[2]
    nslc = t // SLC_BLOCK
    gw = NSA_HPG * NSA_HDIM
    qspec = pl.BlockSpec((1, Q_BLOCK, gw), lambda b, g, i: (b, i, g))
    cspec = pl.BlockSpec((1, 1, ncmp_pad, LANE), lambda b, g, i: (b, g, 0, 0))
    kvspec = pl.BlockSpec((1, t, LANE), lambda b, g, i: (b, 0, g))
    out = jax.ShapeDtypeStruct((bsz, t, NSA_HEADS * NSA_HDIM), F32)
    return pl.pallas_call(
        functools.partial(_nsa_body, t=t),
        grid=(bsz, NSA_KV_GROUPS, t // Q_BLOCK),
        in_specs=[
            pl.BlockSpec(memory_space=pltpu.SMEM),
            qspec, cspec, cspec, kvspec, kvspec, kvspec, kvspec,
            pl.BlockSpec((ncmp_pad, nslc), lambda b, g, i: (0, 0)),
        ],
        out_specs=[qspec, qspec, qspec],
        out_shape=[out, out, out],
        scratch_shapes=[
            pltpu.VMEM((NSA_HPG, Q_BLOCK, 1), F32),
            pltpu.VMEM((NSA_HPG, Q_BLOCK, 1), F32),
            pltpu.VMEM((NSA_HPG, Q_BLOCK, LANE), F32),
        ],
        compiler_params=_cparams(("parallel", "parallel", "arbitrary")),
        name="nsa_attn",
    )(slopes, q, kcmp, vcmp, ks, vs, kw, vw, mcs)


def _outproj1_body(h_ref, oc_ref, os_ref, ow_ref, gt_ref, ex_ref, wo_ref, o_ref):
    gt = gt_ref[...]
    hi = gt.astype(BF16)
    lo = (gt - hi.astype(F32)).astype(BF16)
    o = jnp.zeros(oc_ref.shape, F32)
    for c, br in enumerate((oc_ref, os_ref, ow_ref)):
        ge = _dot(hi, ex_ref[c]) + _dot(lo, ex_ref[c])
        o = o + ge * br[...]
    o_ref[...] = h_ref[...] + _dot(o.astype(BF16), wo_ref[...])


def _outproj1(h, oc, os_, ow, gates, expand, w_out, *, tm=512):
    m = h.shape[0]
    row = pl.BlockSpec((tm, D_MODEL), lambda i: (i, 0))
    return pl.pallas_call(
        _outproj1_body,
        grid=(m // tm,),
        in_specs=[row, row, row, row,
                  pl.BlockSpec((tm, LANE), lambda i: (i, 0)),
                  pl.BlockSpec((3, LANE, D_MODEL), lambda i: (0, 0, 0)),
                  pl.BlockSpec((D_MODEL, D_MODEL), lambda i: (0, 0))],
        out_specs=row,
        out_shape=jax.ShapeDtypeStruct((m, D_MODEL), F32),
        compiler_params=_cparams(("parallel",)),
        name="outproj1",
    )(h, oc, os_, ow, gates, expand, w_out)


def _pad_heads(w, heads, width, padded):
    lead = w.shape[:-1]
    w = w.reshape(lead + (heads, width))
    w = jnp.pad(w, [(0, 0)] * len(lead) + [(0, 0), (0, padded - width)])
    return w.reshape(lead + (heads * padded,))


def _dup_groups(w):
    lead = w.shape[:-1]
    w = w.reshape(lead + (NSA_KV_GROUPS, NSA_HDIM))
    return jnp.concatenate([w, w], axis=-1).reshape(lead + (KV_DUP,))


def _alibi_slopes_grouped():
    s = np.array([2.0 ** (-8.0 * (i + 1) / NSA_HEADS) for i in range(NSA_HEADS)], dtype=np.float32)
    return s.reshape(NSA_HPG, NSA_KV_GROUPS).T.reshape(-1)


def _cmp_to_slc(ncmp_pad, nslc):
    cs = np.arange(ncmp_pad) * CMP_STRIDE
    ss = np.arange(nslc) * SLC_BLOCK
    ov = np.minimum(cs[:, None] + CMP_BLOCK, ss[None] + SLC_BLOCK) - np.maximum(cs[:, None], ss[None])
    return (np.clip(ov, 0, None) / CMP_BLOCK).astype(np.float32)


def _gate_expand():
    e = np.zeros((3, LANE, NSA_HEADS * NSA_HDIM), np.float32)
    for c in range(3):
        for g in range(NSA_KV_GROUPS):
            for h in range(NSA_HPG):
                col = g * NSA_HPG * 3 + h * 3 + c
                base = (g * NSA_HPG + h) * NSA_HDIM
                e[c, col, base:base + NSA_HDIM] = 1.0
    return e


def _compress_weights(w1, w2, pe):
    half = CMP_BLOCK // 2
    eye = jnp.eye(NSA_KV_GROUPS, dtype=w1.dtype)
    w1 = w1.reshape(2, half, NSA_HDIM, CMP_HIDDEN)
    wexp = jnp.einsum('sjdc,gh->sjgdhc', w1, eye).reshape(2, half * NSA_KV, NSA_KV_GROUPS * CMP_HIDDEN)
    pe = jnp.broadcast_to(pe.reshape(2, half, 1, NSA_HDIM), (2, half, NSA_KV_GROUPS, NSA_HDIM)).reshape(2, 1, half * NSA_KV)
    w2d = jnp.concatenate([w2, w2], axis=-1)
    return pe[0], pe[1], wexp[0].astype(BF16), wexp[1].astype(BF16), w2d.astype(BF16)


def _layer0_mixer(h3, norm_mix, a_w_in, a_gate_w2, a_gate_b, a_gla_norm, a_pool_w, a_pool_scale, a_w_out):
    bsz, t, d = h3.shape
    cuts = np.cumsum([GLA_KEY, GLA_KEY, GLA_VAL, GLA_VAL, GLA_GATE_RANK]).tolist()
    wq, wk, wv, wg, wgr, wp = jnp.split(a_w_in, cuts, axis=-1)
    w0 = jnp.concatenate([
        _pad_heads(wq, GLA_HEADS, GLA_DK, GLA_DK_PAD), _pad_heads(wk, GLA_HEADS, GLA_DK, GLA_DK_PAD),
        wv, wg, wp, jnp.pad(wgr, ((0, 0), (0, LANE - GLA_GATE_RANK)))], axis=-1).astype(BF16)
    w2 = jnp.pad(_pad_heads(a_gate_w2, GLA_HEADS, GLA_DK, GLA_DK_PAD), ((0, LANE - GLA_GATE_RANK), (0, 0))).astype(BF16)
    b2 = _pad_heads(a_gate_b, GLA_HEADS, GLA_DK, GLA_DK_PAD).reshape(1, -1)
    q, k, v, g, p, la = _inproj0(h3.reshape(bsz * t, d), norm_mix.reshape(1, d), w0, w2, b2)
    r3 = lambda z: z.reshape(bsz, t, -1)
    oa = _gla(r3(q), r3(k), r3(v), r3(g), r3(la), a_gla_norm.reshape(1, GLA_DV))
    return _outproj0(h3, oa, r3(p), a_pool_w.astype(BF16), a_pool_scale.reshape(1, -1), a_w_out.astype(BF16))


def _layer1_mixer(h3, norm_mix, c_w_in, c_cmp_pe, c_cmpk_w1, c_cmpk_w2, c_cmpv_w1, c_cmpv_w2, c_w_out):
    bsz, t, d = h3.shape
    cuts = (NSA_HEADS * NSA_HDIM + NSA_KV * np.arange(7)).tolist()
    wq, wkc, wvc, wks, wvs, wkw, wvw, wgt = jnp.split(c_w_in, cuts, axis=-1)
    w1 = jnp.concatenate([wq, wkc, wvc, _dup_groups(wks), _dup_groups(wvs), _dup_groups(wkw), _dup_groups(wvw),
                          jnp.pad(wgt, ((0, 0), (0, LANE - wgt.shape[-1])))], axis=-1).astype(BF16)
    q, kc, vc, ks, vs, kw, vw, gates = _inproj1(h3.reshape(bsz * t, d), norm_mix.reshape(1, d), w1)
    nrow = t // CMP_STRIDE
    kcmp = _compress(kc.reshape(bsz, nrow, CMP_STRIDE * NSA_KV), *_compress_weights(c_cmpk_w1, c_cmpk_w2, c_cmp_pe))
    vcmp = _compress(vc.reshape(bsz, nrow, CMP_STRIDE * NSA_KV), *_compress_weights(c_cmpv_w1, c_cmpv_w2, c_cmp_pe))
    r3 = lambda z: z.reshape(bsz, t, -1)
    mcs = jnp.asarray(_cmp_to_slc(nrow, t // SLC_BLOCK)).astype(BF16)
    oc, os_, ow = _nsa_attn(jnp.asarray(_alibi_slopes_grouped()), r3(q), kcmp, vcmp, r3(ks), r3(vs), r3(kw), r3(vw), mcs)
    f2 = lambda z: z.reshape(bsz * t, -1)
    out = _outproj1(h3.reshape(bsz * t, d), f2(oc), f2(os_), f2(ow), gates,
                    jnp.asarray(_gate_expand()).astype(BF16), c_w_out.astype(BF16))
    return out.reshape(bsz, t, d)


def kernel(x, norm_ffn1, ffn1_wg, ffn1_wu, ffn1_wd, norm_mix, norm_ffn2, ffn2_wg, ffn2_wu, ffn2_wd, a_w_in, a_gate_w2, a_gate_b, a_gla_norm, a_pool_w, a_pool_scale, a_w_out, c_w_in, c_cmp_pe, c_cmpk_w1, c_cmpk_w2, c_cmpv_w1, c_cmpv_w2, c_w_out, final_norm):
    bsz, t, d = x.shape
    depth = norm_ffn1.shape[0]
    fg = final_norm.reshape(1, d)
    h = x
    for l in range(depth):
        h = _ffn(h.reshape(bsz * t, d), norm_ffn1[l].reshape(1, d), ffn1_wg[l].astype(BF16), ffn1_wu[l].astype(BF16),
                 ffn1_wd[l].astype(BF16), fg, final_norm=False).reshape(bsz, t, d)
        i = l // 2
        if l % 2 == 0:
            h = _layer0_mixer(h, norm_mix[l], a_w_in[i], a_gate_w2[i], a_gate_b[i], a_gla_norm[i],
                              a_pool_w[i], a_pool_scale[i], a_w_out[i])
        else:
            h = _layer1_mixer(h, norm_mix[l], c_w_in[i], c_cmp_pe[i], c_cmpk_w1[i], c_cmpk_w2[i],
                              c_cmpv_w1[i], c_cmpv_w2[i], c_w_out[i])
        h = _ffn(h.reshape(bsz * t, d), norm_ffn2[l].reshape(1, d), ffn2_wg[l].astype(BF16), ffn2_wu[l].astype(BF16),
                 ffn2_wd[l].astype(BF16), fg, final_norm=(l == depth - 1)).reshape(bsz, t, d)
    return h
```

```python
import functools
import math

import numpy as np
import jax
import jax.numpy as jnp
from jax import lax
from jax.experimental import pallas as pl
from jax.experimental.pallas import tpu as pltpu

F32 = jnp.float32
BF16 = jnp.bfloat16

D_MODEL = 1024
D_FF = 2816
EPS = 1e-6

GLA_HEADS = 4
GLA_DV = 128
GLA_DK = 64
GLA_DK_PAD = 128
GLA_KEY = GLA_HEADS * GLA_DK
GLA_VAL = GLA_HEADS * GLA_DV
GLA_GATE_RANK = 16
GLA_TAU = 16.0
GLA_CHUNK = 64

POOL_WINDOWS = (2, 4, 8, 16)
POOL_GROUPS = 4
POOL_GDIM = 128
POOL_WIDTH = POOL_GROUPS * POOL_GDIM
POOL_HALO = 16

NSA_HDIM = 64
NSA_HEADS = 16
NSA_KV_GROUPS = 2
NSA_HPG = NSA_HEADS // NSA_KV_GROUPS
NSA_PAIRS = NSA_HPG // 2
NSA_KV = NSA_KV_GROUPS * NSA_HDIM
CMP_BLOCK = 32
CMP_STRIDE = 16
CMP_HIDDEN = 256
SLC_BLOCK = 64
SLC_SHIFT = 6
SLC_TOP = 16
WINDOW = 512
Q_BLOCK = 128
NEG = -1e30
FORCE = 1e4

LANE = 128
VMEM_LIMIT = 56 * 1024 * 1024

MAX_SLC = LANE
FEAT0 = NSA_HDIM
STAGE_BLOCKS = 8
ROWS = NSA_HPG * Q_BLOCK


def _cparams(sem):
    return pltpu.CompilerParams(dimension_semantics=sem, vmem_limit_bytes=VMEM_LIMIT)


def _rmsnorm(x, g):
    return x * lax.rsqrt(jnp.mean(x * x, axis=-1, keepdims=True) + EPS) * g


def _dot(a, b):
    return jnp.dot(a, b, preferred_element_type=F32)


def _dot_nt(a, b):
    return lax.dot_general(a, b, (((1,), (1,)), ((), ())), preferred_element_type=F32)


def _split3(x):
    hi = x.astype(BF16)
    r1 = x - hi.astype(F32)
    mid = r1.astype(BF16)
    lo = (r1 - mid.astype(F32)).astype(BF16)
    return hi, mid, lo


def _ffn_body(h_ref, g_ref, wg_ref, wu_ref, wd_ref, fg_ref, o_ref, xn_ref, acc_ref, *, final_norm):
    j = pl.program_id(1)

    @pl.when(j == 0)
    def _():
        xn_ref[...] = _rmsnorm(h_ref[...], g_ref[...]).astype(BF16)
        acc_ref[...] = jnp.zeros_like(acc_ref)

    xn = xn_ref[...]
    a = _dot(xn, wg_ref[...])
    u = _dot(xn, wu_ref[...])
    hid = (a * jax.nn.sigmoid(a) * u).astype(BF16)
    acc_ref[...] += _dot(hid, wd_ref[...])

    @pl.when(j == pl.num_programs(1) - 1)
    def _():
        y = h_ref[...] + 0.5 * acc_ref[...]
        if final_norm:
            y = _rmsnorm(y, fg_ref[...])
        o_ref[...] = y


def _ffn(h, gain, wg, wu, wd, final_gain, *, final_norm, tm=1024, tf=256):
    m = h.shape[0]
    assert m % tm == 0 and D_FF % tf == 0
    return pl.pallas_call(
        functools.partial(_ffn_body, final_norm=final_norm),
        grid=(m // tm, D_FF // tf),
        in_specs=[
            pl.BlockSpec((tm, D_MODEL), lambda i, j: (i, 0)),
            pl.BlockSpec((1, D_MODEL), lambda i, j: (0, 0)),
            pl.BlockSpec((D_MODEL, tf), lambda i, j: (0, j)),
            pl.BlockSpec((D_MODEL, tf), lambda i, j: (0, j)),
            pl.BlockSpec((tf, D_MODEL), lambda i, j: (j, 0)),
            pl.BlockSpec((1, D_MODEL), lambda i, j: (0, 0)),
        ],
        out_specs=pl.BlockSpec((tm, D_MODEL), lambda i, j: (i, 0)),
        out_shape=jax.ShapeDtypeStruct((m, D_MODEL), F32),
        scratch_shapes=[pltpu.VMEM((tm, D_MODEL), BF16), pltpu.VMEM((tm, D_MODEL), F32)],
        compiler_params=_cparams(("parallel", "arbitrary")),
        name="ffn",
    )(h, gain, wg, wu, wd, final_gain)


W0_Q, W0_K, W0_V, W0_G, W0_P, W0_GR = 0, 512, 1024, 1536, 2048, 2560
W0_COLS = 2688


def _inproj0_body(h_ref, g_ref, w_ref, w2_ref, b2_ref, q_ref, k_ref, v_ref, gg_ref, p_ref, la_ref):
    xn = _rmsnorm(h_ref[...], g_ref[...]).astype(BF16)
    q_ref[...] = _dot(xn, w_ref[:, W0_Q:W0_Q + 512])
    k_ref[...] = _dot(xn, w_ref[:, W0_K:W0_K + 512])
    v_ref[...] = _dot(xn, w_ref[:, W0_V:W0_V + 512])
    gg_ref[...] = _dot(xn, w_ref[:, W0_G:W0_G + 512])
    p_ref[...] = _dot(xn, w_ref[:, W0_P:W0_P + 512])
    gr = _dot(xn, w_ref[:, W0_GR:W0_GR + LANE]).astype(BF16)
    z = _dot(gr, w2_ref[...]) + b2_ref[...]
    log_sig = jnp.minimum(z, 0.0) - jnp.log1p(jnp.exp(-jnp.abs(z)))
    la_ref[...] = log_sig / GLA_TAU


def _inproj0(h, gain, w0, w2, b2, *, tm=512):
    m = h.shape[0]
    wide = pl.BlockSpec((tm, 512), lambda i: (i, 0))
    return pl.pallas_call(
        _inproj0_body,
        grid=(m // tm,),
        in_specs=[
            pl.BlockSpec((tm, D_MODEL), lambda i: (i, 0)),
            pl.BlockSpec((1, D_MODEL), lambda i: (0, 0)),
            pl.BlockSpec((D_MODEL, W0_COLS), lambda i: (0, 0)),
            pl.BlockSpec((LANE, 512), lambda i: (0, 0)),
            pl.BlockSpec((1, 512), lambda i: (0, 0)),
        ],
        out_specs=[wide] * 6,
        out_shape=[jax.ShapeDtypeStruct((m, 512), F32)] * 6,
        compiler_params=_cparams(("parallel",)),
        name="inproj0",
    )(h, gain, w0, w2, b2)


def _gla_body(q_ref, k_ref, v_ref, g_ref, la_ref, gn_ref, o_ref, st_ref, *, tt):
    c = GLA_CHUNK
    nchunk = tt // c

    @pl.when(pl.program_id(2) == 0)
    def _():
        st_ref[...] = jnp.zeros_like(st_ref)

    la = la_ref[0]
    row = lax.broadcasted_iota(jnp.int32, (tt, tt), 0)
    col = lax.broadcasted_iota(jnp.int32, (tt, tt), 1)
    shift = int(math.log2(c))
    same_chunk = (row >> shift) == (col >> shift)
    cum_m = jnp.where(same_chunk & (col <= row), 1.0, 0.0).astype(BF16)
    tot_m = jnp.where(same_chunk, 1.0, 0.0).astype(BF16)
    la3 = _split3(la)
    b = sum(_dot(cum_m, t) for t in la3)
    b_last = sum(_dot(tot_m, t) for t in la3)

    q = q_ref[0] * (GLA_DK ** -0.5)
    k = k_ref[0]
    q_dec = (q * jnp.exp(b)).astype(BF16)
    k_inv = (k * jnp.exp(-b)).astype(BF16)
    k_end = (k * jnp.exp(b_last - b)).astype(BF16)
    decay = jnp.exp(b_last)
    v = v_ref[0]
    v_bf = v.astype(BF16)

    ci = lax.broadcasted_iota(jnp.int32, (c, c), 0)
    cj = lax.broadcasted_iota(jnp.int32, (c, c), 1)
    causal = cj <= ci

    st = st_ref[...]
    outs = []
    for n in range(nchunk):
        sl = slice(n * c, (n + 1) * c)
        att = jnp.where(causal, _dot_nt(q_dec[sl], k_inv[sl]), 0.0)
        o = _dot(att.astype(BF16), v_bf[sl]) + _dot_nt(q_dec[sl], st.astype(BF16))
        outs.append(o)
        d_state = _dot(v[sl].T.astype(BF16), k_end[sl])
        st = decay[n * c:n * c + 1, :] * st + d_state
    st_ref[...] = st

    o = jnp.concatenate(outs, axis=0)
    o = o * lax.rsqrt(jnp.mean(o * o, axis=-1, keepdims=True) + EPS) * gn_ref[...]
    g = g_ref[0]
    o_ref[0] = o * (g * jax.nn.sigmoid(g))


def _gla(q, k, v, g, la, gn, *, tt=512):
    bsz, t, _ = q.shape
    blk = pl.BlockSpec((1, tt, LANE), lambda b, h, i: (b, i, h))
    return pl.pallas_call(
        functools.partial(_gla_body, tt=tt),
        grid=(bsz, GLA_HEADS, t // tt),
        in_specs=[blk, blk, blk, blk, blk, pl.BlockSpec((1, LANE), lambda b, h, i: (0, 0))],
        out_specs=blk,
        out_shape=jax.ShapeDtypeStruct((bsz, t, GLA_VAL), F32),
        scratch_shapes=[pltpu.VMEM((GLA_DV, GLA_DK_PAD), F32)],
        compiler_params=_cparams(("parallel", "parallel", "arbitrary")),
        name="gla",
    )(q, k, v, g, la, gn)


def _outproj0_body(h_ref, oa_ref, p_ref, pw_ref, ps_ref, wo_ref, o_ref, pbuf_ref, *, tm):
    i = pl.program_id(1)

    @pl.when(i == 0)
    def _():
        pbuf_ref[0:POOL_HALO, :] = jnp.zeros((POOL_HALO, POOL_WIDTH), F32)

    p = p_ref[0]
    pbuf_ref[POOL_HALO:POOL_HALO + tm, :] = p
    tpos = i * tm + lax.broadcasted_iota(jnp.int32, (tm, 1), 0)
    mix = _dot(oa_ref[0].astype(BF16), wo_ref[0:GLA_VAL, :])
    for gi, w in enumerate(POOL_WINDOWS):
        ls = slice(gi * POOL_GDIM, (gi + 1) * POOL_GDIM)
        acc = p[:, ls]
        for s in range(1, w):
            acc = acc + pbuf_ref[POOL_HALO - s:POOL_HALO - s + tm, ls]
        cnt = jnp.minimum(tpos + 1, w).astype(F32)
        pooled = acc / cnt - p[:, ls]
        ob = _dot(pooled.astype(BF16), pw_ref[gi]) * ps_ref[:, ls]
        mix = mix + _dot(ob.astype(BF16), wo_ref[GLA_VAL + gi * POOL_GDIM:GLA_VAL + (gi + 1) * POOL_GDIM, :])
    pbuf_ref[0:POOL_HALO, :] = p[tm - POOL_HALO:, :]
    o_ref[0] = h_ref[0] + mix


def _outproj0(h, oa, p, pool_w, pool_scale, w_out, *, tm=512):
    bsz, t, _ = h.shape
    return pl.pallas_call(
        functools.partial(_outproj0_body, tm=tm),
        grid=(bsz, t // tm),
        in_specs=[
            pl.BlockSpec((1, tm, D_MODEL), lambda b, i: (b, i, 0)),
            pl.BlockSpec((1, tm, GLA_VAL), lambda b, i: (b, i, 0)),
            pl.BlockSpec((1, tm, POOL_WIDTH), lambda b, i: (b, i, 0)),
            pl.BlockSpec((POOL_GROUPS, POOL_GDIM, POOL_GDIM), lambda b, i: (0, 0, 0)),
            pl.BlockSpec((1, POOL_WIDTH), lambda b, i: (0, 0)),
            pl.BlockSpec((D_MODEL, D_MODEL), lambda b, i: (0, 0)),
        ],
        out_specs=pl.BlockSpec((1, tm, D_MODEL), lambda b, i: (b, i, 0)),
        out_shape=jax.ShapeDtypeStruct((bsz, t, D_MODEL), F32),
        scratch_shapes=[pltpu.VMEM((POOL_HALO + tm, POOL_WIDTH), F32)],
        compiler_params=_cparams(("parallel", "arbitrary")),
        name="outproj0",
    )(h, oa, p, pool_w, pool_scale, w_out)


KEY_W = 2 * LANE
VAL_W = 2 * LANE
W1_Q, W1_KC, W1_VC, W1_KS, W1_VS, W1_KW, W1_VW, W1_GT = 0, 1024, 1152, 1280, 1536, 1792, 2048, 2304
W1_COLS = 2432


def _inproj1_body(h_ref, g_ref, w_ref, kf_ref, q_ref, kc_ref, vc_ref, ks_ref, vs_ref, kw_ref, vw_ref, gt_ref):
    xn = _rmsnorm(h_ref[...], g_ref[...]).astype(BF16)
    tm = xn.shape[0]
    q_ref[...] = (_dot(xn, w_ref[:, W1_Q:W1_Q + 1024]) * (NSA_HDIM ** -0.5)).astype(BF16)
    kc_ref[...] = _dot(xn, w_ref[:, W1_KC:W1_KC + NSA_KV])
    vc_ref[...] = _dot(xn, w_ref[:, W1_VC:W1_VC + NSA_KV])
    feat = kf_ref[:, 0:LANE].astype(F32)
    onehot = kf_ref[:, LANE:2 * LANE]
    ones = jnp.ones((tm, LANE), BF16)
    for g in range(NSA_KV_GROUPS):
        gl = slice(g * LANE, (g + 1) * LANE)
        ks = (_dot(xn, w_ref[:, W1_KS + g * LANE:W1_KS + (g + 1) * LANE]) + feat).astype(BF16)
        ks_ref[:, g * KEY_W:g * KEY_W + LANE] = ks
        ks_ref[:, g * KEY_W + LANE:(g + 1) * KEY_W] = onehot
        kw_ref[:, gl] = (_dot(xn, w_ref[:, W1_KW + g * LANE:W1_KW + (g + 1) * LANE]) + feat).astype(BF16)
        vs_ref[:, g * VAL_W:g * VAL_W + LANE] = _dot(xn, w_ref[:, W1_VS + g * LANE:W1_VS + (g + 1) * LANE]).astype(BF16)
        vs_ref[:, g * VAL_W + LANE:(g + 1) * VAL_W] = ones
        vw_ref[:, g * VAL_W:g * VAL_W + LANE] = _dot(xn, w_ref[:, W1_VW + g * LANE:W1_VW + (g + 1) * LANE]).astype(BF16)
        vw_ref[:, g * VAL_W + LANE:(g + 1) * VAL_W] = ones
    gt_ref[...] = jax.nn.sigmoid(_dot(xn, w_ref[:, W1_GT:W1_GT + LANE]))


def _inproj1(h, gain, w1, kfeat, *, t, tm=512):
    m = h.shape[0]
    per_seq = t // tm
    spec = lambda n: pl.BlockSpec((tm, n), lambda i: (i, 0))
    shp = lambda n, dt: jax.ShapeDtypeStruct((m, n), dt)
    g = NSA_KV_GROUPS
    return pl.pallas_call(
        _inproj1_body,
        grid=(m // tm,),
        in_specs=[
            pl.BlockSpec((tm, D_MODEL), lambda i: (i, 0)),
            pl.BlockSpec((1, D_MODEL), lambda i: (0, 0)),
            pl.BlockSpec((D_MODEL, W1_COLS), lambda i: (0, 0)),
            pl.BlockSpec((tm, KEY_W), lambda i: (i % per_seq, 0)),
        ],
        out_specs=[spec(1024), spec(NSA_KV), spec(NSA_KV), spec(g * KEY_W), spec(g * VAL_W), spec(g * LANE),
                   spec(g * VAL_W), spec(LANE)],
        out_shape=[shp(1024, BF16), shp(NSA_KV, F32), shp(NSA_KV, F32), shp(g * KEY_W, BF16), shp(g * VAL_W, BF16),
                   shp(g * LANE, BF16), shp(g * VAL_W, BF16), shp(LANE, F32)],
        compiler_params=_cparams(("parallel",)),
        name="inproj1",
    )(h, gain, w1, kfeat)


def _compress_body(z_ref, pea_ref, peb_ref, wa_ref, wb_ref, w2_ref, cf_ref, o_ref, sh_ref, *, nrow):
    z = z_ref[0]
    a = _dot((z + pea_ref[...]).astype(BF16), wa_ref[...])
    bm = _dot((z + peb_ref[...]).astype(BF16), wb_ref[...])
    sh_ref[0:nrow, :] = bm
    sh_ref[nrow:nrow + 8, :] = jnp.zeros((8, sh_ref.shape[1]), F32)
    x = a + sh_ref[1:nrow + 1, :]
    cdf = 0.5 * (1.0 + jnp.tanh(math.sqrt(2.0 / math.pi) * (x + 0.044715 * (x * x * x))))
    hid = (x * cdf).astype(BF16)
    for g in range(NSA_KV_GROUPS):
        o_ref[0, g] = (_dot(hid[:, g * CMP_HIDDEN:(g + 1) * CMP_HIDDEN], w2_ref[...]) + cf_ref[...]).astype(BF16)


def _compress(z, pea, peb, wa, wb, w2, cfeat):
    bsz, nrow, zw = z.shape
    hw = NSA_KV_GROUPS * CMP_HIDDEN
    return pl.pallas_call(
        functools.partial(_compress_body, nrow=nrow),
        grid=(bsz,),
        in_specs=[
            pl.BlockSpec((1, nrow, zw), lambda b: (b, 0, 0)),
            pl.BlockSpec((1, zw), lambda b: (0, 0)),
            pl.BlockSpec((1, zw), lambda b: (0, 0)),
            pl.BlockSpec((zw, hw), lambda b: (0, 0)),
            pl.BlockSpec((zw, hw), lambda b: (0, 0)),
            pl.BlockSpec((CMP_HIDDEN, LANE), lambda b: (0, 0)),
            pl.BlockSpec((nrow, LANE), lambda b: (0, 0)),
        ],
        out_specs=pl.BlockSpec((1, NSA_KV_GROUPS, nrow, LANE), lambda b: (b, 0, 0, 0)),
        out_shape=jax.ShapeDtypeStruct((bsz, NSA_KV_GROUPS, nrow, LANE), BF16),
        scratch_shapes=[pltpu.VMEM((nrow + 8, hw), F32)],
        compiler_params=_cparams(("parallel",)),
        name="compress",
    )(z, pea, peb, wa, wb, w2, cfeat)


def _fill_query_rows(qa_ref, q_ref, qf_ref):
    lane_lo = lax.broadcasted_iota(jnp.int32, (Q_BLOCK, LANE), 1) < NSA_HDIM
    for pr in range(NSA_PAIRS):
        even = q_ref[0, :, pr * LANE:(pr + 1) * LANE]
        odd = pltpu.roll(even.astype(F32), NSA_HDIM, axis=1).astype(BF16)
        for hh, src in enumerate((even, odd)):
            h = 2 * pr + hh
            feat = jnp.broadcast_to(qf_ref[0, 0, h:h + 1, :], (Q_BLOCK, LANE)).astype(BF16)
            qa_ref[h * Q_BLOCK:(h + 1) * Q_BLOCK, 0:LANE] = jnp.where(lane_lo, src, feat)


def _merge_pairs(o_ref, o):
    lane_lo = lax.broadcasted_iota(jnp.int32, (Q_BLOCK, LANE), 1) < NSA_HDIM
    for pr in range(NSA_PAIRS):
        a = o[(2 * pr) * Q_BLOCK:(2 * pr + 1) * Q_BLOCK]
        b = o[(2 * pr + 1) * Q_BLOCK:(2 * pr + 2) * Q_BLOCK]
        o_ref[0, :, pr * LANE:(pr + 1) * LANE] = jnp.where(lane_lo, a, b)


def _head_tile(bias):
    return jnp.concatenate([bias] * NSA_HPG, axis=0)


def _nsa_select_body(qf_ref, q_ref, kc_ref, vc_ref, mcst_ref, oc_ref, sb_ref, lst_ref, qa_ref):
    qi = pl.program_id(2)
    t0 = qi * Q_BLOCK
    ncmp_pad = kc_ref.shape[2]
    nq = Q_BLOCK
    _fill_query_rows(qa_ref, q_ref, qf_ref)

    tpos = t0 + lax.broadcasted_iota(jnp.int32, (nq, 1), 0)
    cmp_end = lax.broadcasted_iota(jnp.int32, (1, ncmp_pad), 1) * CMP_STRIDE + (CMP_BLOCK - 1)
    bias_c = jnp.where(tpos >= cmp_end, 0.0, NEG)
    any_c = jnp.where(tpos >= (CMP_BLOCK - 1), 1.0, 0.0)
    s = _dot_nt(qa_ref[...], kc_ref[0, 0]) + _head_tile(bias_c)
    e = jnp.exp(s - jnp.max(s, axis=-1, keepdims=True))
    p = e * (_head_tile(any_c) / jnp.sum(e, axis=-1, keepdims=True))
    _merge_pairs(oc_ref, _dot(p.astype(BF16), vc_ref[0, 0]))
    p_sum = p[0:nq]
    for h in range(1, NSA_HPG):
        p_sum = p_sum + p[h * nq:(h + 1) * nq]

    imp = sum(_dot_nt(mcst_ref[...], term) for term in _split3(p_sum))
    blk = lax.broadcasted_iota(jnp.int32, (MAX_SLC, nq), 0)
    tpos_l = t0 + lax.broadcasted_iota(jnp.int32, (MAX_SLC, nq), 1)
    cur = tpos_l >> SLC_SHIFT
    forced = (blk == 0) | (blk == cur) | (blk == cur - 1)
    valid_b = blk * SLC_BLOCK <= tpos_l
    score = jnp.where(valid_b, jnp.where(forced, FORCE, imp), -1.0)
    blk_f = blk.astype(F32)
    sel = jnp.zeros((MAX_SLC, nq), F32)
    for _ in range(SLC_TOP):
        mx = jnp.max(score, axis=0, keepdims=True)
        first = jnp.min(jnp.where(score == mx, blk_f, float(MAX_SLC)), axis=0, keepdims=True)
        pick = blk_f == first
        sel = jnp.where(pick, 1.0, sel)
        score = jnp.where(pick, -2.0, score)
    sel = jnp.where(valid_b & (blk < 2 * qi), sel, 0.0)
    sb_ref[0, 0] = jnp.where(sel.T > 0.5, 0.0, NEG).astype(BF16)

    used = jnp.max(sel, axis=1, keepdims=True)
    used_b = jnp.broadcast_to(used, (MAX_SLC, LANE)).astype(BF16)
    r = lax.broadcasted_iota(jnp.int32, (MAX_SLC, LANE), 0)
    c = lax.broadcasted_iota(jnp.int32, (MAX_SLC, LANE), 1)
    before = jnp.where(c < r, 1.0, 0.0).astype(BF16)
    pos = _dot(before, used_b)
    place = jnp.where((pos == c.astype(F32)) & (used_b > 0.5), 1.0, 0.0).astype(BF16)
    ids = lax.broadcasted_iota(jnp.int32, (8, LANE), 1).astype(F32).astype(BF16)
    lst = _dot(ids, place)
    cnt = _dot(jnp.ones((8, LANE), BF16), used_b)
    slot = lax.broadcasted_iota(jnp.int32, (8, LANE), 1)
    lst = jnp.where(slot.astype(F32) < cnt, lst, (2 * qi).astype(F32))
    row0 = lax.broadcasted_iota(jnp.int32, (8, LANE), 0) == 0
    lst_ref[0, 0, 0] = jnp.where(row0, lst, cnt).astype(jnp.int32)


def _nsa_select(qfeat, q, kcmp, vcmp, mcst):
    bsz, t, _ = q.shape
    ncmp_pad = kcmp.shape[2]
    nqt = t // Q_BLOCK
    gw = NSA_HPG * NSA_HDIM
    qspec = pl.BlockSpec((1, Q_BLOCK, gw), lambda b, g, i: (b, i, g))
    cspec = pl.BlockSpec((1, 1, ncmp_pad, LANE), lambda b, g, i: (b, g, 0, 0))
    return pl.pallas_call(
        _nsa_select_body,
        grid=(bsz, NSA_KV_GROUPS, nqt),
        in_specs=[
            pl.BlockSpec((1, 1, NSA_HPG, LANE), lambda b, g, i: (g, i, 0, 0)),
            qspec, cspec, cspec,
            pl.BlockSpec((MAX_SLC, ncmp_pad), lambda b, g, i: (0, 0)),
        ],
        out_specs=[qspec,
                   pl.BlockSpec((1, 1, Q_BLOCK, MAX_SLC), lambda b, g, i: (b, g, i, 0)),
                   pl.BlockSpec((1, 1, 1, 8, LANE), lambda b, g, i: (b, g, i, 0, 0))],
        out_shape=[jax.ShapeDtypeStruct((bsz, t, NSA_HEADS * NSA_HDIM), F32),
                   jax.ShapeDtypeStruct((bsz, NSA_KV_GROUPS, t, MAX_SLC), BF16),
                   jax.ShapeDtypeStruct((bsz, NSA_KV_GROUPS, nqt, 8, LANE), jnp.int32)],
        scratch_shapes=[pltpu.VMEM((ROWS, LANE), BF16)],
        compiler_params=_cparams(("parallel", "parallel", "parallel")),
        name="nsa_select",
    )(qfeat, q, kcmp, vcmp, mcst)


def _nsa_attend_body(lst_ref, cnt_ref, qf_ref, q_ref, sb_ref, ks_ref, vs_ref, kw_ref, vw_ref, os_ref, ow_ref,
                     qa_ref, kst_ref, vst_ref, m_ref, acc_ref):
    b = pl.program_id(0)
    g = pl.program_id(1)
    qi = pl.program_id(2)
    nqt = pl.num_programs(2)
    t0 = pl.multiple_of(qi * Q_BLOCK, Q_BLOCK)
    nq = Q_BLOCK
    _fill_query_rows(qa_ref, q_ref, qf_ref)
    sb = sb_ref[0, 0]
    for h in range(NSA_HPG):
        qa_ref[h * nq:(h + 1) * nq, LANE:2 * LANE] = sb
    qa_lo = qa_ref[:, 0:LANE]

    ti = lax.broadcasted_iota(jnp.int32, (nq, nq), 0)
    tj = lax.broadcasted_iota(jnp.int32, (nq, nq), 1)

    s = _dot_nt(qa_lo, ks_ref[0, pl.ds(t0, nq), 0:LANE]) + _head_tile(jnp.where(tj <= ti, 0.0, NEG))
    m = jnp.max(s, axis=-1, keepdims=True)
    m_ref[...] = m
    acc_ref[...] = _dot(jnp.exp(s - m).astype(BF16), vs_ref[0, pl.ds(t0, nq), :])

    tile = (b * NSA_KV_GROUPS + g) * nqt + qi
    base = tile * LANE
    count = cnt_ref[tile]

    def step(c, carry):
        for j in range(STAGE_BLOCKS):
            r0 = pl.multiple_of(lst_ref[base + c * STAGE_BLOCKS + j] * SLC_BLOCK, SLC_BLOCK)
            kst_ref[j * SLC_BLOCK:(j + 1) * SLC_BLOCK, :] = ks_ref[0, pl.ds(r0, SLC_BLOCK), :]
            vst_ref[j * SLC_BLOCK:(j + 1) * SLC_BLOCK, :] = vs_ref[0, pl.ds(r0, SLC_BLOCK), :]
        s = _dot_nt(qa_ref[...], kst_ref[...])
        m_old = m_ref[...]
        m_new = jnp.maximum(m_old, jnp.max(s, axis=-1, keepdims=True))
        acc_ref[...] = jnp.exp(m_old - m_new) * acc_ref[...] + _dot(jnp.exp(s - m_new).astype(BF16), vst_ref[...])
        m_ref[...] = m_new
        return carry

    lax.fori_loop(0, (count + STAGE_BLOCKS - 1) // STAGE_BLOCKS, step, 0)
    acc = acc_ref[...]
    _merge_pairs(os_ref, acc[:, 0:LANE] / acc[:, LANE:2 * LANE])

    wk = WINDOW + nq
    w0 = pl.multiple_of(jnp.maximum(t0 - WINDOW, 0), nq)
    dist = (t0 + lax.broadcasted_iota(jnp.int32, (nq, 1), 0)) - (w0 + lax.broadcasted_iota(jnp.int32, (1, wk), 1))
    bias_w = jnp.where((dist >= 0) & (dist < WINDOW), 0.0, NEG)
    s = _dot_nt(qa_lo, kw_ref[0, pl.ds(w0, wk), :]) + _head_tile(bias_w)
    e = jnp.exp(s - jnp.max(s, axis=-1, keepdims=True))
    acc = _dot(e.astype(BF16), vw_ref[0, pl.ds(w0, wk), :])
    _merge_pairs(ow_ref, acc[:, 0:LANE] / acc[:, LANE:2 * LANE])


def _nsa_attend(lists, counts, qfeat, q, selbias, ks, vs, kw, vw):
    bsz, t, _ = q.shape
    nqt = t // Q_BLOCK
    gw = NSA_HPG * NSA_HDIM
    qspec = pl.BlockSpec((1, Q_BLOCK, gw), lambda b, g, i, lst, cnt: (b, i, g))
    out = jax.ShapeDtypeStruct((bsz, t, NSA_HEADS * NSA_HDIM), F32)
    stage = STAGE_BLOCKS * SLC_BLOCK
    return pl.pallas_call(
        _nsa_attend_body,
        grid_spec=pltpu.PrefetchScalarGridSpec(
            num_scalar_prefetch=2,
            grid=(bsz, NSA_KV_GROUPS, nqt),
            in_specs=[
                pl.BlockSpec((1, 1, NSA_HPG, LANE), lambda b, g, i, lst, cnt: (g, i, 0, 0)),
                qspec,
                pl.BlockSpec((1, 1, Q_BLOCK, MAX_SLC), lambda b, g, i, lst, cnt: (b, g, i, 0)),
                pl.BlockSpec((1, t, KEY_W), lambda b, g, i, lst, cnt: (b, 0, g)),
                pl.BlockSpec((1, t, VAL_W), lambda b, g, i, lst, cnt: (b, 0, g)),
                pl.BlockSpec((1, t, LANE), lambda b, g, i, lst, cnt: (b, 0, g)),
                pl.BlockSpec((1, t, VAL_W), lambda b, g, i, lst, cnt: (b, 0, g)),
            ],
            out_specs=[qspec, qspec],
            scratch_shapes=[
                pltpu.VMEM((ROWS, KEY_W), BF16),
                pltpu.VMEM((stage, KEY_W), BF16),
                pltpu.VMEM((stage, VAL_W), BF16),
                pltpu.VMEM((ROWS, 1), F32),
                pltpu.VMEM((ROWS, VAL_W), F32),
            ],
        ),
        out_shape=[out, out],
        compiler_params=_cparams(("parallel", "parallel", "arbitrary")),
        name="nsa_attend",
    )(lists, counts, qfeat, q, selbias, ks, vs, kw, vw)


def _outproj1_body(h_ref, oc_ref, os_ref, ow_ref, gt_ref, ex_ref, wo_ref, o_ref):
    gt = gt_ref[...]
    hi = gt.astype(BF16)
    lo = (gt - hi.astype(F32)).astype(BF16)
    o = jnp.zeros(oc_ref.shape, F32)
    for c, br in enumerate((oc_ref, os_ref, ow_ref)):
        ge = _dot(hi, ex_ref[c]) + _dot(lo, ex_ref[c])
        o = o + ge * br[...]
    o_ref[...] = h_ref[...] + _dot(o.astype(BF16), wo_ref[...])


def _outproj1(h, oc, os_, ow, gates, expand, w_out, *, tm=512):
    m = h.shape[0]
    row = pl.BlockSpec((tm, D_MODEL), lambda i: (i, 0))
    return pl.pallas_call(
        _outproj1_body,
        grid=(m // tm,),
        in_specs=[row, row, row, row,
                  pl.BlockSpec((tm, LANE), lambda i: (i, 0)),
                  pl.BlockSpec((3, LANE, D_MODEL), lambda i: (0, 0, 0)),
                  pl.BlockSpec((D_MODEL, D_MODEL), lambda i: (0, 0))],
        out_specs=row,
        out_shape=jax.ShapeDtypeStruct((m, D_MODEL), F32),
        compiler_params=_cparams(("parallel",)),
        name="outproj1",
    )(h, oc, os_, ow, gates, expand, w_out)


def _pad_heads(w, heads, width, padded):
    lead = w.shape[:-1]
    w = w.reshape(lead + (heads, width))
    w = jnp.pad(w, [(0, 0)] * len(lead) + [(0, 0), (0, padded - width)])
    return w.reshape(lead + (heads * padded,))


def _dup_groups(w):
    lead = w.shape[:-1]
    w = w.reshape(lead + (NSA_KV_GROUPS, NSA_HDIM))
    return jnp.concatenate([w, w], axis=-1).reshape(lead + (NSA_KV_GROUPS * LANE,))


def _np_split3(x):
    out = []
    r = np.asarray(x, np.float64)
    for _ in range(3):
        part = np.asarray(np.asarray(r, np.float32).astype(jnp.bfloat16), np.float64)
        out.append(part.astype(np.float32))
        r = r - part
    return out


def _position_terms(pos):
    pos = np.asarray(pos)
    f = np.zeros((pos.shape[0], LANE), np.float32)
    f[:, FEAT0 + 0:FEAT0 + 3] = (pos >> SLC_SHIFT)[:, None]
    f[:, FEAT0 + 3:FEAT0 + 6] = (pos & (SLC_BLOCK - 1))[:, None]
    f[:, FEAT0 + 6:FEAT0 + 9] = 1.0
    return f


def _slope_terms(nqt):
    slopes = np.array([2.0 ** (-8.0 * (i + 1) / NSA_HEADS) for i in range(NSA_HEADS)], dtype=np.float32)
    slopes = slopes.reshape(NSA_HPG, NSA_KV_GROUPS).T.astype(np.float64)
    f = np.zeros((NSA_KV_GROUPS, nqt, NSA_HPG, LANE), np.float32)
    t0 = (np.arange(nqt) * Q_BLOCK).astype(np.float64)
    for k, part in enumerate(_np_split3(slopes * SLC_BLOCK)):
        f[:, :, :, FEAT0 + k] = part[:, None, :]
    for k, part in enumerate(_np_split3(slopes)):
        f[:, :, :, FEAT0 + 3 + k] = part[:, None, :]
    for k, part in enumerate(_np_split3(-slopes[:, None, :] * t0[None, :, None])):
        f[:, :, :, FEAT0 + 6 + k] = part
    return f


def _key_terms(t):
    pos = np.arange(t)
    onehot = np.zeros((t, MAX_SLC), np.float32)
    onehot[pos, pos >> SLC_SHIFT] = 1.0
    return np.concatenate([_position_terms(pos), onehot], axis=1)


def _cmp_to_slc_t(ncmp_pad):
    cs = np.arange(ncmp_pad) * CMP_STRIDE
    ss = np.arange(MAX_SLC) * SLC_BLOCK
    ov = np.minimum(cs[None] + CMP_BLOCK, ss[:, None] + SLC_BLOCK) - np.maximum(cs[None], ss[:, None])
    return (np.clip(ov, 0, None) / CMP_BLOCK).astype(np.float32)


def _gate_expand():
    e = np.zeros((3, LANE, NSA_HEADS * NSA_HDIM), np.float32)
    for c in range(3):
        for g in range(NSA_KV_GROUPS):
            for h in range(NSA_HPG):
                col = g * NSA_HPG * 3 + h * 3 + c
                base = (g * NSA_HPG + h) * NSA_HDIM
                e[c, col, base:base + NSA_HDIM] = 1.0
    return e


def _compress_weights(w1, w2, pe, dup):
    half = CMP_BLOCK // 2
    eye = jnp.eye(NSA_KV_GROUPS, dtype=w1.dtype)
    w1 = w1.reshape(2, half, NSA_HDIM, CMP_HIDDEN)
    wexp = jnp.einsum('sjdc,gh->sjgdhc', w1, eye).reshape(2, half * NSA_KV, NSA_KV_GROUPS * CMP_HIDDEN)
    pe = jnp.broadcast_to(pe.reshape(2, half, 1, NSA_HDIM), (2, half, NSA_KV_GROUPS, NSA_HDIM)).reshape(2, 1, half * NSA_KV)
    w2w = jnp.concatenate([w2, w2 if dup else jnp.zeros_like(w2)], axis=-1)
    return pe[0], pe[1], wexp[0].astype(BF16), wexp[1].astype(BF16), w2w.astype(BF16)


def _layer0_mixer(h3, norm_mix, a_w_in, a_gate_w2, a_gate_b, a_gla_norm, a_pool_w, a_pool_scale, a_w_out):
    bsz, t, d = h3.shape
    cuts = np.cumsum([GLA_KEY, GLA_KEY, GLA_VAL, GLA_VAL, GLA_GATE_RANK]).tolist()
    wq, wk, wv, wg, wgr, wp = jnp.split(a_w_in, cuts, axis=-1)
    w0 = jnp.concatenate([
        _pad_heads(wq, GLA_HEADS, GLA_DK, GLA_DK_PAD), _pad_heads(wk, GLA_HEADS, GLA_DK, GLA_DK_PAD),
        wv, wg, wp, jnp.pad(wgr, ((0, 0), (0, LANE - GLA_GATE_RANK)))], axis=-1).astype(BF16)
    w2 = jnp.pad(_pad_heads(a_gate_w2, GLA_HEADS, GLA_DK, GLA_DK_PAD), ((0, LANE - GLA_GATE_RANK), (0, 0))).astype(BF16)
    b2 = _pad_heads(a_gate_b, GLA_HEADS, GLA_DK, GLA_DK_PAD).reshape(1, -1)
    q, k, v, g, p, la = _inproj0(h3.reshape(bsz * t, d), norm_mix.reshape(1, d), w0, w2, b2)
    r3 = lambda z: z.reshape(bsz, t, -1)
    oa = _gla(r3(q), r3(k), r3(v), r3(g), r3(la), a_gla_norm.reshape(1, GLA_DV))
    return _outproj0(h3, oa, r3(p), a_pool_w.astype(BF16), a_pool_scale.reshape(1, -1), a_w_out.astype(BF16))


def _layer1_mixer(h3, norm_mix, c_w_in, c_cmp_pe, c_cmpk_w1, c_cmpk_w2, c_cmpv_w1, c_cmpv_w2, c_w_out):
    bsz, t, d = h3.shape
    assert t // SLC_BLOCK <= MAX_SLC and t >= WINDOW + Q_BLOCK
    nqt = t // Q_BLOCK
    cuts = (NSA_HEADS * NSA_HDIM + NSA_KV * np.arange(7)).tolist()
    wq, wkc, wvc, wks, wvs, wkw, wvw, wgt = jnp.split(c_w_in, cuts, axis=-1)
    w1 = jnp.concatenate([wq, wkc, wvc, _pad_heads(wks, NSA_KV_GROUPS, NSA_HDIM, LANE), _dup_groups(wvs),
                          _pad_heads(wkw, NSA_KV_GROUPS, NSA_HDIM, LANE), _dup_groups(wvw),
                          jnp.pad(wgt, ((0, 0), (0, LANE - wgt.shape[-1])))], axis=-1).astype(BF16)
    kfeat = jnp.asarray(_key_terms(t)).astype(BF16)
    q, kc, vc, ks, vs, kw, vw, gates = _inproj1(h3.reshape(bsz * t, d), norm_mix.reshape(1, d), w1, kfeat, t=t)
    nrow = t // CMP_STRIDE
    cfeat = jnp.asarray(_position_terms(np.arange(nrow) * CMP_STRIDE + CMP_BLOCK - 1))
    zero = jnp.zeros_like(cfeat)
    kcmp = _compress(kc.reshape(bsz, nrow, CMP_STRIDE * NSA_KV),
                     *_compress_weights(c_cmpk_w1, c_cmpk_w2, c_cmp_pe, dup=False), cfeat)
    vcmp = _compress(vc.reshape(bsz, nrow, CMP_STRIDE * NSA_KV),
                     *_compress_weights(c_cmpv_w1, c_cmpv_w2, c_cmp_pe, dup=True), zero)
    r3 = lambda z: z.reshape(bsz, t, -1)
    qfeat = jnp.asarray(_slope_terms(nqt))
    mcst = jnp.asarray(_cmp_to_slc_t(nrow)).astype(BF16)
    oc, selbias, lists = _nsa_select(qfeat, r3(q), kcmp, vcmp, mcst)
    os_, ow = _nsa_attend(lists[:, :, :, 0, :].reshape(-1), lists[:, :, :, 1, 0].reshape(-1), qfeat, r3(q), selbias,
                          r3(ks), r3(vs), r3(kw), r3(vw))
    f2 = lambda z: z.reshape(bsz * t, -1)
    out = _outproj1(h3.reshape(bsz * t, d), f2(oc), f2(os_), f2(ow), gates,
                    jnp.asarray(_gate_expand()).astype(BF16), c_w_out.astype(BF16))
    return out.reshape(bsz, t, d)


def kernel(x, norm_ffn1, ffn1_wg, ffn1_wu, ffn1_wd, norm_mix, norm_ffn2, ffn2_wg, ffn2_wu, ffn2_wd, a_w_in, a_gate_w2, a_gate_b, a_gla_norm, a_pool_w, a_pool_scale, a_w_out, c_w_in, c_cmp_pe, c_cmpk_w1, c_cmpk_w2, c_cmpv_w1, c_cmpv_w2, c_w_out, final_norm):
    bsz, t, d = x.shape
    depth = norm_ffn1.shape[0]
    fg = final_norm.reshape(1, d)
    h = x
    for l in range(depth):
        h = _ffn(h.reshape(bsz * t, d), norm_ffn1[l].reshape(1, d), ffn1_wg[l].astype(BF16), ffn1_wu[l].astype(BF16),
                 ffn1_wd[l].astype(BF16), fg, final_norm=False).reshape(bsz, t, d)
        i = l // 2
        if l % 2 == 0:
            h = _layer0_mixer(h, norm_mix[l], a_w_in[i], a_gate_w2[i], a_gate_b[i], a_gla_norm[i],
                              a_pool_w[i], a_pool_scale[i], a_w_out[i])
        else:
            h = _layer1_mixer(h, norm_mix[l], c_w_in[i], c_cmp_pe[i], c_cmpk_w1[i], c_cmpk_w2[i],
                              c_cmpv_w1[i], c_cmpv_w2[i], c_w_out[i])
        h = _ffn(h.reshape(bsz * t, d), norm_ffn2[l].reshape(1, d), ffn2_wg[l].astype(BF16), ffn2_wu[l].astype(BF16),
                 ffn2_wd[l].astype(BF16), fg, final_norm=(l == depth - 1)).reshape(bsz, t, d)
    return h
```

```python
import functools
import math

import numpy as np
import jax
import jax.numpy as jnp
from jax import lax
from jax.experimental import pallas as pl
from jax.experimental.pallas import tpu as pltpu

F32 = jnp.float32
BF16 = jnp.bfloat16

D_MODEL = 1024
D_FF = 2816
EPS = 1e-6

GLA_HEADS = 4
GLA_DV = 128
GLA_DK = 64
GLA_DK_PAD = 128
GLA_KEY = GLA_HEADS * GLA_DK
GLA_VAL = GLA_HEADS * GLA_DV
GLA_GATE_RANK = 16
GLA_TAU = 16.0
GLA_CHUNK = 64

POOL_WINDOWS = (2, 4, 8, 16)
POOL_GROUPS = 4
POOL_GDIM = 128
POOL_WIDTH = POOL_GROUPS * POOL_GDIM
POOL_HALO = 16

NSA_HDIM = 64
NSA_HEADS = 16
NSA_KV_GROUPS = 2
NSA_HPG = NSA_HEADS // NSA_KV_GROUPS
NSA_PAIRS = NSA_HPG // 2
NSA_KV = NSA_KV_GROUPS * NSA_HDIM
CMP_BLOCK = 32
CMP_STRIDE = 16
CMP_HIDDEN = 256
SLC_BLOCK = 64
SLC_SHIFT = 6
SLC_TOP = 16
WINDOW = 512
Q_BLOCK = 128
NEG = -1e30
FORCE = 1e4

LANE = 128
VMEM_LIMIT = 56 * 1024 * 1024

MAX_SLC = LANE
FEAT0 = NSA_HDIM
STAGE_BLOCKS = 8
ROWS = NSA_HPG * Q_BLOCK


def _cparams(sem):
    return pltpu.CompilerParams(dimension_semantics=sem, vmem_limit_bytes=VMEM_LIMIT)


def _rmsnorm(x, g):
    return x * lax.rsqrt(jnp.mean(x * x, axis=-1, keepdims=True) + EPS) * g


def _dot(a, b):
    return jnp.dot(a, b, preferred_element_type=F32)


def _dot_nt(a, b):
    return lax.dot_general(a, b, (((1,), (1,)), ((), ())), preferred_element_type=F32)


def _split3(x):
    hi = x.astype(BF16)
    r1 = x - hi.astype(F32)
    mid = r1.astype(BF16)
    lo = (r1 - mid.astype(F32)).astype(BF16)
    return hi, mid, lo


def _ffn_body(h_ref, g_ref, wg_ref, wu_ref, wd_ref, fg_ref, o_ref, xn_ref, acc_ref, *, final_norm):
    j = pl.program_id(1)

    @pl.when(j == 0)
    def _():
        xn_ref[...] = _rmsnorm(h_ref[...], g_ref[...]).astype(BF16)
        acc_ref[...] = jnp.zeros_like(acc_ref)

    xn = xn_ref[...]
    a = _dot(xn, wg_ref[...])
    u = _dot(xn, wu_ref[...])
    hid = (a * jax.nn.sigmoid(a) * u).astype(BF16)
    acc_ref[...] += _dot(hid, wd_ref[...])

    @pl.when(j == pl.num_programs(1) - 1)
    def _():
        y = h_ref[...] + 0.5 * acc_ref[...]
        if final_norm:
            y = _rmsnorm(y, fg_ref[...])
        o_ref[...] = y


def _ffn(h, gain, wg, wu, wd, final_gain, *, final_norm, tm=1024, tf=256):
    m = h.shape[0]
    assert m % tm == 0 and D_FF % tf == 0
    return pl.pallas_call(
        functools.partial(_ffn_body, final_norm=final_norm),
        grid=(m // tm, D_FF // tf),
        in_specs=[
            pl.BlockSpec((tm, D_MODEL), lambda i, j: (i, 0)),
            pl.BlockSpec((1, D_MODEL), lambda i, j: (0, 0)),
            pl.BlockSpec((D_MODEL, tf), lambda i, j: (0, j)),
            pl.BlockSpec((D_MODEL, tf), lambda i, j: (0, j)),
            pl.BlockSpec((tf, D_MODEL), lambda i, j: (j, 0)),
            pl.BlockSpec((1, D_MODEL), lambda i, j: (0, 0)),
        ],
        out_specs=pl.BlockSpec((tm, D_MODEL), lambda i, j: (i, 0)),
        out_shape=jax.ShapeDtypeStruct((m, D_MODEL), F32),
        scratch_shapes=[pltpu.VMEM((tm, D_MODEL), BF16), pltpu.VMEM((tm, D_MODEL), F32)],
        compiler_params=_cparams(("parallel", "arbitrary")),
        name="ffn",
    )(h, gain, wg, wu, wd, final_gain)


W0_Q, W0_K, W0_V, W0_G, W0_P, W0_GR = 0, 512, 1024, 1536, 2048, 2560
W0_COLS = 2688


def _inproj0_body(h_ref, g_ref, w_ref, w2_ref, b2_ref, q_ref, k_ref, v_ref, gg_ref, p_ref, la_ref):
    xn = _rmsnorm(h_ref[...], g_ref[...]).astype(BF16)
    q_ref[...] = _dot(xn, w_ref[:, W0_Q:W0_Q + 512])
    k_ref[...] = _dot(xn, w_ref[:, W0_K:W0_K + 512])
    v_ref[...] = _dot(xn, w_ref[:, W0_V:W0_V + 512])
    gg_ref[...] = _dot(xn, w_ref[:, W0_G:W0_G + 512])
    p_ref[...] = _dot(xn, w_ref[:, W0_P:W0_P + 512])
    gr = _dot(xn, w_ref[:, W0_GR:W0_GR + LANE]).astype(BF16)
    z = _dot(gr, w2_ref[...]) + b2_ref[...]
    log_sig = jnp.minimum(z, 0.0) - jnp.log1p(jnp.exp(-jnp.abs(z)))
    la_ref[...] = log_sig / GLA_TAU


def _inproj0(h, gain, w0, w2, b2, *, tm=512):
    m = h.shape[0]
    wide = pl.BlockSpec((tm, 512), lambda i: (i, 0))
    return pl.pallas_call(
        _inproj0_body,
        grid=(m // tm,),
        in_specs=[
            pl.BlockSpec((tm, D_MODEL), lambda i: (i, 0)),
            pl.BlockSpec((1, D_MODEL), lambda i: (0, 0)),
            pl.BlockSpec((D_MODEL, W0_COLS), lambda i: (0, 0)),
            pl.BlockSpec((LANE, 512), lambda i: (0, 0)),
            pl.BlockSpec((1, 512), lambda i: (0, 0)),
        ],
        out_specs=[wide] * 6,
        out_shape=[jax.ShapeDtypeStruct((m, 512), F32)] * 6,
        compiler_params=_cparams(("parallel",)),
        name="inproj0",
    )(h, gain, w0, w2, b2)


def _gla_body(q_ref, k_ref, v_ref, g_ref, la_ref, gn_ref, o_ref, st_ref, *, tt):
    c = GLA_CHUNK
    nchunk = tt // c

    @pl.when(pl.program_id(2) == 0)
    def _():
        st_ref[...] = jnp.zeros_like(st_ref)

    la = la_ref[0]
    row = lax.broadcasted_iota(jnp.int32, (tt, tt), 0)
    col = lax.broadcasted_iota(jnp.int32, (tt, tt), 1)
    shift = int(math.log2(c))
    same_chunk = (row >> shift) == (col >> shift)
    cum_m = jnp.where(same_chunk & (col <= row), 1.0, 0.0).astype(BF16)
    tot_m = jnp.where(same_chunk, 1.0, 0.0).astype(BF16)
    la3 = _split3(la)
    b = sum(_dot(cum_m, t) for t in la3)
    b_last = sum(_dot(tot_m, t) for t in la3)

    q = q_ref[0] * (GLA_DK ** -0.5)
    k = k_ref[0]
    q_dec = (q * jnp.exp(b)).astype(BF16)
    k_inv = (k * jnp.exp(-b)).astype(BF16)
    k_end = (k * jnp.exp(b_last - b)).astype(BF16)
    decay = jnp.exp(b_last)
    v = v_ref[0]
    v_bf = v.astype(BF16)

    ci = lax.broadcasted_iota(jnp.int32, (c, c), 0)
    cj = lax.broadcasted_iota(jnp.int32, (c, c), 1)
    causal = cj <= ci

    st = st_ref[...]
    outs = []
    for n in range(nchunk):
        sl = slice(n * c, (n + 1) * c)
        att = jnp.where(causal, _dot_nt(q_dec[sl], k_inv[sl]), 0.0)
        o = _dot(att.astype(BF16), v_bf[sl]) + _dot_nt(q_dec[sl], st.astype(BF16))
        outs.append(o)
        d_state = _dot(v[sl].T.astype(BF16), k_end[sl])
        st = decay[n * c:n * c + 1, :] * st + d_state
    st_ref[...] = st

    o = jnp.concatenate(outs, axis=0)
    o = o * lax.rsqrt(jnp.mean(o * o, axis=-1, keepdims=True) + EPS) * gn_ref[...]
    g = g_ref[0]
    o_ref[0] = o * (g * jax.nn.sigmoid(g))


def _gla(q, k, v, g, la, gn, *, tt=512):
    bsz, t, _ = q.shape
    blk = pl.BlockSpec((1, tt, LANE), lambda b, h, i: (b, i, h))
    return pl.pallas_call(
        functools.partial(_gla_body, tt=tt),
        grid=(bsz, GLA_HEADS, t // tt),
        in_specs=[blk, blk, blk, blk, blk, pl.BlockSpec((1, LANE), lambda b, h, i: (0, 0))],
        out_specs=blk,
        out_shape=jax.ShapeDtypeStruct((bsz, t, GLA_VAL), F32),
        scratch_shapes=[pltpu.VMEM((GLA_DV, GLA_DK_PAD), F32)],
        compiler_params=_cparams(("parallel", "parallel", "arbitrary")),
        name="gla",
    )(q, k, v, g, la, gn)


def _outproj0_body(h_ref, oa_ref, p_ref, pw_ref, ps_ref, wo_ref, o_ref, pbuf_ref, *, tm):
    i = pl.program_id(1)

    @pl.when(i == 0)
    def _():
        pbuf_ref[0:POOL_HALO, :] = jnp.zeros((POOL_HALO, POOL_WIDTH), F32)

    p = p_ref[0]
    pbuf_ref[POOL_HALO:POOL_HALO + tm, :] = p
    tpos = i * tm + lax.broadcasted_iota(jnp.int32, (tm, 1), 0)
    mix = _dot(oa_ref[0].astype(BF16), wo_ref[0:GLA_VAL, :])
    for gi, w in enumerate(POOL_WINDOWS):
        ls = slice(gi * POOL_GDIM, (gi + 1) * POOL_GDIM)
        acc = p[:, ls]
        for s in range(1, w):
            acc = acc + pbuf_ref[POOL_HALO - s:POOL_HALO - s + tm, ls]
        cnt = jnp.minimum(tpos + 1, w).astype(F32)
        pooled = acc / cnt - p[:, ls]
        ob = _dot(pooled.astype(BF16), pw_ref[gi]) * ps_ref[:, ls]
        mix = mix + _dot(ob.astype(BF16), wo_ref[GLA_VAL + gi * POOL_GDIM:GLA_VAL + (gi + 1) * POOL_GDIM, :])
    pbuf_ref[0:POOL_HALO, :] = p[tm - POOL_HALO:, :]
    o_ref[0] = h_ref[0] + mix


def _outproj0(h, oa, p, pool_w, pool_scale, w_out, *, tm=512):
    bsz, t, _ = h.shape
    return pl.pallas_call(
        functools.partial(_outproj0_body, tm=tm),
        grid=(bsz, t // tm),
        in_specs=[
            pl.BlockSpec((1, tm, D_MODEL), lambda b, i: (b, i, 0)),
            pl.BlockSpec((1, tm, GLA_VAL), lambda b, i: (b, i, 0)),
            pl.BlockSpec((1, tm, POOL_WIDTH), lambda b, i: (b, i, 0)),
            pl.BlockSpec((POOL_GROUPS, POOL_GDIM, POOL_GDIM), lambda b, i: (0, 0, 0)),
            pl.BlockSpec((1, POOL_WIDTH), lambda b, i: (0, 0)),
            pl.BlockSpec((D_MODEL, D_MODEL), lambda b, i: (0, 0)),
        ],
        out_specs=pl.BlockSpec((1, tm, D_MODEL), lambda b, i: (b, i, 0)),
        out_shape=jax.ShapeDtypeStruct((bsz, t, D_MODEL), F32),
        scratch_shapes=[pltpu.VMEM((POOL_HALO + tm, POOL_WIDTH), F32)],
        compiler_params=_cparams(("parallel", "arbitrary")),
        name="outproj0",
    )(h, oa, p, pool_w, pool_scale, w_out)


KEY_W = 2 * LANE
VAL_W = 2 * LANE
W1_Q, W1_KC, W1_VC, W1_KS, W1_VS, W1_KW, W1_VW, W1_GT = 0, 1024, 1152, 1280, 1536, 1792, 2048, 2304
W1_COLS = 2432


def _inproj1_body(h_ref, g_ref, w_ref, kf_ref, q_ref, kc_ref, vc_ref, ks_ref, vs_ref, kw_ref, vw_ref, gt_ref):
    xn = _rmsnorm(h_ref[...], g_ref[...]).astype(BF16)
    tm = xn.shape[0]
    q_ref[...] = (_dot(xn, w_ref[:, W1_Q:W1_Q + 1024]) * (NSA_HDIM ** -0.5)).astype(BF16)
    kc_ref[...] = _dot(xn, w_ref[:, W1_KC:W1_KC + NSA_KV])
    vc_ref[...] = _dot(xn, w_ref[:, W1_VC:W1_VC + NSA_KV])
    feat = kf_ref[:, 0:LANE].astype(F32)
    onehot = kf_ref[:, LANE:2 * LANE]
    ones = jnp.ones((tm, LANE), BF16)
    for g in range(NSA_KV_GROUPS):
        gl = slice(g * LANE, (g + 1) * LANE)
        ks = (_dot(xn, w_ref[:, W1_KS + g * LANE:W1_KS + (g + 1) * LANE]) + feat).astype(BF16)
        ks_ref[:, g * KEY_W:g * KEY_W + LANE] = ks
        ks_ref[:, g * KEY_W + LANE:(g + 1) * KEY_W] = onehot
        kw_ref[:, gl] = (_dot(xn, w_ref[:, W1_KW + g * LANE:W1_KW + (g + 1) * LANE]) + feat).astype(BF16)
        vs_ref[:, g * VAL_W:g * VAL_W + LANE] = _dot(xn, w_ref[:, W1_VS + g * LANE:W1_VS + (g + 1) * LANE]).astype(BF16)
        vs_ref[:, g * VAL_W + LANE:(g + 1) * VAL_W] = ones
        vw_ref[:, g * VAL_W:g * VAL_W + LANE] = _dot(xn, w_ref[:, W1_VW + g * LANE:W1_VW + (g + 1) * LANE]).astype(BF16)
        vw_ref[:, g * VAL_W + LANE:(g + 1) * VAL_W] = ones
    gt_ref[...] = jax.nn.sigmoid(_dot(xn, w_ref[:, W1_GT:W1_GT + LANE]))


def _inproj1(h, gain, w1, kfeat, *, t, tm=512):
    m = h.shape[0]
    per_seq = t // tm
    spec = lambda n: pl.BlockSpec((tm, n), lambda i: (i, 0))
    shp = lambda n, dt: jax.ShapeDtypeStruct((m, n), dt)
    g = NSA_KV_GROUPS
    return pl.pallas_call(
        _inproj1_body,
        grid=(m // tm,),
        in_specs=[
            pl.BlockSpec((tm, D_MODEL), lambda i: (i, 0)),
            pl.BlockSpec((1, D_MODEL), lambda i: (0, 0)),
            pl.BlockSpec((D_MODEL, W1_COLS), lambda i: (0, 0)),
            pl.BlockSpec((tm, KEY_W), lambda i: (i % per_seq, 0)),
        ],
        out_specs=[spec(1024), spec(NSA_KV), spec(NSA_KV), spec(g * KEY_W), spec(g * VAL_W), spec(g * LANE),
                   spec(g * VAL_W), spec(LANE)],
        out_shape=[shp(1024, BF16), shp(NSA_KV, F32), shp(NSA_KV, F32), shp(g * KEY_W, BF16), shp(g * VAL_W, BF16),
                   shp(g * LANE, BF16), shp(g * VAL_W, BF16), shp(LANE, F32)],
        compiler_params=_cparams(("parallel",)),
        name="inproj1",
    )(h, gain, w1, kfeat)


def _compress_body(z_ref, pea_ref, peb_ref, wa_ref, wb_ref, w2_ref, cf_ref, o_ref, sh_ref, *, nrow):
    z = z_ref[0]
    a = _dot((z + pea_ref[...]).astype(BF16), wa_ref[...])
    bm = _dot((z + peb_ref[...]).astype(BF16), wb_ref[...])
    sh_ref[0:nrow, :] = bm
    sh_ref[nrow:nrow + 8, :] = jnp.zeros((8, sh_ref.shape[1]), F32)
    x = a + sh_ref[1:nrow + 1, :]
    cdf = 0.5 * (1.0 + jnp.tanh(math.sqrt(2.0 / math.pi) * (x + 0.044715 * (x * x * x))))
    hid = (x * cdf).astype(BF16)
    for g in range(NSA_KV_GROUPS):
        o_ref[0, g] = (_dot(hid[:, g * CMP_HIDDEN:(g + 1) * CMP_HIDDEN], w2_ref[...]) + cf_ref[...]).astype(BF16)


def _compress(z, pea, peb, wa, wb, w2, cfeat):
    bsz, nrow, zw = z.shape
    hw = NSA_KV_GROUPS * CMP_HIDDEN
    return pl.pallas_call(
        functools.partial(_compress_body, nrow=nrow),
        grid=(bsz,),
        in_specs=[
            pl.BlockSpec((1, nrow, zw), lambda b: (b, 0, 0)),
            pl.BlockSpec((1, zw), lambda b: (0, 0)),
            pl.BlockSpec((1, zw), lambda b: (0, 0)),
            pl.BlockSpec((zw, hw), lambda b: (0, 0)),
            pl.BlockSpec((zw, hw), lambda b: (0, 0)),
            pl.BlockSpec((CMP_HIDDEN, LANE), lambda b: (0, 0)),
            pl.BlockSpec((nrow, LANE), lambda b: (0, 0)),
        ],
        out_specs=pl.BlockSpec((1, NSA_KV_GROUPS, nrow, LANE), lambda b: (b, 0, 0, 0)),
        out_shape=jax.ShapeDtypeStruct((bsz, NSA_KV_GROUPS, nrow, LANE), BF16),
        scratch_shapes=[pltpu.VMEM((nrow + 8, hw), F32)],
        compiler_params=_cparams(("parallel",)),
        name="compress",
    )(z, pea, peb, wa, wb, w2, cfeat)


def _fill_query_rows(qa_ref, q_ref, qf_ref):
    lane_lo = lax.broadcasted_iota(jnp.int32, (Q_BLOCK, LANE), 1) < NSA_HDIM
    for pr in range(NSA_PAIRS):
        even = q_ref[0, :, pr * LANE:(pr + 1) * LANE]
        odd = pltpu.roll(even.astype(F32), NSA_HDIM, axis=1).astype(BF16)
        for hh, src in enumerate((even, odd)):
            h = 2 * pr + hh
            feat = jnp.broadcast_to(qf_ref[0, 0, h:h + 1, :], (Q_BLOCK, LANE)).astype(BF16)
            qa_ref[h * Q_BLOCK:(h + 1) * Q_BLOCK, 0:LANE] = jnp.where(lane_lo, src, feat)


def _merge_pairs(o_ref, o):
    lane_lo = lax.broadcasted_iota(jnp.int32, (Q_BLOCK, LANE), 1) < NSA_HDIM
    for pr in range(NSA_PAIRS):
        a = o[(2 * pr) * Q_BLOCK:(2 * pr + 1) * Q_BLOCK]
        b = o[(2 * pr + 1) * Q_BLOCK:(2 * pr + 2) * Q_BLOCK]
        o_ref[0, :, pr * LANE:(pr + 1) * LANE] = jnp.where(lane_lo, a, b)


def _head_tile(bias):
    return jnp.concatenate([bias] * NSA_HPG, axis=0)


def _nsa_select_body(qf_ref, q_ref, kc_ref, vc_ref, mcst_ref, oc_ref, sb_ref, lst_ref, qa_ref):
    qi = pl.program_id(2)
    t0 = qi * Q_BLOCK
    ncmp_pad = kc_ref.shape[2]
    nq = Q_BLOCK
    _fill_query_rows(qa_ref, q_ref, qf_ref)

    tpos = t0 + lax.broadcasted_iota(jnp.int32, (nq, 1), 0)
    cmp_end = lax.broadcasted_iota(jnp.int32, (1, ncmp_pad), 1) * CMP_STRIDE + (CMP_BLOCK - 1)
    bias_c = jnp.where(tpos >= cmp_end, 0.0, NEG)
    any_c = jnp.where(tpos >= (CMP_BLOCK - 1), 1.0, 0.0)
    s = _dot_nt(qa_ref[...], kc_ref[0, 0]) + _head_tile(bias_c)
    e = jnp.exp(s - jnp.max(s, axis=-1, keepdims=True))
    p = e * (_head_tile(any_c) / jnp.sum(e, axis=-1, keepdims=True))
    _merge_pairs(oc_ref, _dot(p.astype(BF16), vc_ref[0, 0]))
    p_sum = p[0:nq]
    for h in range(1, NSA_HPG):
        p_sum = p_sum + p[h * nq:(h + 1) * nq]

    imp = sum(_dot_nt(mcst_ref[...], term) for term in _split3(p_sum))
    blk = lax.broadcasted_iota(jnp.int32, (MAX_SLC, nq), 0)
    tpos_l = t0 + lax.broadcasted_iota(jnp.int32, (MAX_SLC, nq), 1)
    cur = tpos_l >> SLC_SHIFT
    forced = (blk == 0) | (blk == cur) | (blk == cur - 1)
    valid_b = blk * SLC_BLOCK <= tpos_l
    score = jnp.where(valid_b, jnp.where(forced, FORCE, imp), -1.0)
    blk_f = blk.astype(F32)
    sel = jnp.zeros((MAX_SLC, nq), F32)
    for _ in range(SLC_TOP):
        mx = jnp.max(score, axis=0, keepdims=True)
        first = jnp.min(jnp.where(score == mx, blk_f, float(MAX_SLC)), axis=0, keepdims=True)
        pick = blk_f == first
        sel = jnp.where(pick, 1.0, sel)
        score = jnp.where(pick, -2.0, score)
    sel = jnp.where(valid_b & (blk < 2 * qi), sel, 0.0)
    sb_ref[0, 0] = jnp.where(sel.T > 0.5, 0.0, NEG).astype(BF16)

    used = jnp.max(sel, axis=1, keepdims=True)
    used_b = jnp.broadcast_to(used, (MAX_SLC, LANE)).astype(BF16)
    r = lax.broadcasted_iota(jnp.int32, (MAX_SLC, LANE), 0)
    c = lax.broadcasted_iota(jnp.int32, (MAX_SLC, LANE), 1)
    before = jnp.where(c < r, 1.0, 0.0).astype(BF16)
    pos = _dot(before, used_b)
    place = jnp.where((pos == c.astype(F32)) & (used_b > 0.5), 1.0, 0.0).astype(BF16)
    ids = lax.broadcasted_iota(jnp.int32, (8, LANE), 1).astype(F32).astype(BF16)
    lst = _dot(ids, place)
    cnt = _dot(jnp.ones((8, LANE), BF16), used_b)
    slot = lax.broadcasted_iota(jnp.int32, (8, LANE), 1)
    lst = jnp.where(slot.astype(F32) < cnt, lst, (2 * qi).astype(F32))
    row0 = lax.broadcasted_iota(jnp.int32, (8, LANE), 0) == 0
    lst_ref[0, 0, 0] = jnp.where(row0, lst, cnt).astype(jnp.int32)


def _nsa_select(qfeat, q, kcmp, vcmp, mcst):
    bsz, t, _ = q.shape
    ncmp_pad = kcmp.shape[2]
    nqt = t // Q_BLOCK
    gw = NSA_HPG * NSA_HDIM
    qspec = pl.BlockSpec((1, Q_BLOCK, gw), lambda b, g, i: (b, i, g))
    cspec = pl.BlockSpec((1, 1, ncmp_pad, LANE), lambda b, g, i: (b, g, 0, 0))
    return pl.pallas_call(
        _nsa_select_body,
        grid=(bsz, NSA_KV_GROUPS, nqt),
        in_specs=[
            pl.BlockSpec((1, 1, NSA_HPG, LANE), lambda b, g, i: (g, i, 0, 0)),
            qspec, cspec, cspec,
            pl.BlockSpec((MAX_SLC, ncmp_pad), lambda b, g, i: (0, 0)),
        ],
        out_specs=[qspec,
                   pl.BlockSpec((1, 1, Q_BLOCK, MAX_SLC), lambda b, g, i: (b, g, i, 0)),
                   pl.BlockSpec((1, 1, 1, 8, LANE), lambda b, g, i: (b, g, i, 0, 0))],
        out_shape=[jax.ShapeDtypeStruct((bsz, t, NSA_HEADS * NSA_HDIM), F32),
                   jax.ShapeDtypeStruct((bsz, NSA_KV_GROUPS, t, MAX_SLC), BF16),
                   jax.ShapeDtypeStruct((bsz, NSA_KV_GROUPS, nqt, 8, LANE), jnp.int32)],
        scratch_shapes=[pltpu.VMEM((ROWS, LANE), BF16)],
        compiler_params=_cparams(("parallel", "parallel", "parallel")),
        name="nsa_select",
    )(qfeat, q, kcmp, vcmp, mcst)


def _nsa_attend_body(lst_ref, cnt_ref, qf_ref, q_ref, sb_ref, ks_ref, vs_ref, kw_ref, vw_ref, os_ref, ow_ref,
                     qa_ref, kst0_ref, kst1_ref, vst0_ref, vst1_ref, s0_ref, s1_ref, m_ref, acc_ref):
    b = pl.program_id(0)
    g = pl.program_id(1)
    qi = pl.program_id(2)
    nqt = pl.num_programs(2)
    t0 = pl.multiple_of(qi * Q_BLOCK, Q_BLOCK)
    nq = Q_BLOCK
    _fill_query_rows(qa_ref, q_ref, qf_ref)
    sb = sb_ref[0, 0]
    for h in range(NSA_HPG):
        qa_ref[h * nq:(h + 1) * nq, LANE:2 * LANE] = sb
    qa_lo = qa_ref[:, 0:LANE]

    tile = (b * NSA_KV_GROUPS + g) * nqt + qi
    base = tile * LANE
    count = cnt_ref[tile]

    bufs = ((kst0_ref, vst0_ref, s0_ref), (kst1_ref, vst1_ref, s1_ref))

    def stage_and_score(c, slot):
        kst_ref, vst_ref, s_ref = bufs[slot]
        for j in range(STAGE_BLOCKS):
            idx = jnp.minimum(c * STAGE_BLOCKS + j, LANE - 1)
            r0 = pl.multiple_of(lst_ref[base + idx] * SLC_BLOCK, SLC_BLOCK)
            kst_ref[j * SLC_BLOCK:(j + 1) * SLC_BLOCK, :] = ks_ref[0, pl.ds(r0, SLC_BLOCK), :]
            vst_ref[j * SLC_BLOCK:(j + 1) * SLC_BLOCK, :] = vs_ref[0, pl.ds(r0, SLC_BLOCK), :]
        s_ref[...] = _dot_nt(qa_ref[...], kst_ref[...])

    def softmax_pv(slot):
        _, vst_ref, s_ref = bufs[slot]
        s = s_ref[...]
        m_old = m_ref[...]
        m_new = jnp.maximum(m_old, jnp.max(s, axis=-1, keepdims=True))
        acc_ref[...] = jnp.exp(m_old - m_new) * acc_ref[...] + _dot(jnp.exp(s - m_new).astype(BF16), vst_ref[...])
        m_ref[...] = m_new

    wk = WINDOW + nq
    w0 = pl.multiple_of(jnp.maximum(t0 - WINDOW, 0), nq)
    dist = (t0 + lax.broadcasted_iota(jnp.int32, (nq, 1), 0)) - (w0 + lax.broadcasted_iota(jnp.int32, (1, wk), 1))
    bias_w = jnp.where((dist >= 0) & (dist < WINDOW), 0.0, NEG)
    s = _dot_nt(qa_lo, kw_ref[0, pl.ds(w0, wk), :]) + _head_tile(bias_w)
    e = jnp.exp(s - jnp.max(s, axis=-1, keepdims=True))
    acc = _dot(e.astype(BF16), vw_ref[0, pl.ds(w0, wk), :])
    _merge_pairs(ow_ref, acc[:, 0:LANE] / acc[:, LANE:2 * LANE])

    ti = lax.broadcasted_iota(jnp.int32, (nq, nq), 0)
    tj = lax.broadcasted_iota(jnp.int32, (nq, nq), 1)
    s = _dot_nt(qa_lo, ks_ref[0, pl.ds(t0, nq), 0:LANE]) + _head_tile(jnp.where(tj <= ti, 0.0, NEG))
    m = jnp.max(s, axis=-1, keepdims=True)
    m_ref[...] = m
    acc_ref[...] = _dot(jnp.exp(s - m).astype(BF16), vs_ref[0, pl.ds(t0, nq), :])

    stage_and_score(0, 0)

    n_groups = (count + STAGE_BLOCKS - 1) // STAGE_BLOCKS

    def step(c, carry):
        stage_and_score(2 * c + 1, 1)
        softmax_pv(0)

        @pl.when(2 * c + 1 < n_groups)
        def _():
            stage_and_score(2 * c + 2, 0)
            softmax_pv(1)

        return carry

    lax.fori_loop(0, (n_groups + 1) // 2, step, 0)
    acc = acc_ref[...]
    _merge_pairs(os_ref, acc[:, 0:LANE] / acc[:, LANE:2 * LANE])


def _nsa_attend(lists, counts, qfeat, q, selbias, ks, vs, kw, vw):
    bsz, t, _ = q.shape
    nqt = t // Q_BLOCK
    gw = NSA_HPG * NSA_HDIM
    qspec = pl.BlockSpec((1, Q_BLOCK, gw), lambda b, g, i, lst, cnt: (b, i, g))
    out = jax.ShapeDtypeStruct((bsz, t, NSA_HEADS * NSA_HDIM), F32)
    stage = STAGE_BLOCKS * SLC_BLOCK
    return pl.pallas_call(
        _nsa_attend_body,
        grid_spec=pltpu.PrefetchScalarGridSpec(
            num_scalar_prefetch=2,
            grid=(bsz, NSA_KV_GROUPS, nqt),
            in_specs=[
                pl.BlockSpec((1, 1, NSA_HPG, LANE), lambda b, g, i, lst, cnt: (g, i, 0, 0)),
                qspec,
                pl.BlockSpec((1, 1, Q_BLOCK, MAX_SLC), lambda b, g, i, lst, cnt: (b, g, i, 0)),
                pl.BlockSpec((1, t, KEY_W), lambda b, g, i, lst, cnt: (b, 0, g)),
                pl.BlockSpec((1, t, VAL_W), lambda b, g, i, lst, cnt: (b, 0, g)),
                pl.BlockSpec((1, t, LANE), lambda b, g, i, lst, cnt: (b, 0, g)),
                pl.BlockSpec((1, t, VAL_W), lambda b, g, i, lst, cnt: (b, 0, g)),
            ],
            out_specs=[qspec, qspec],
            scratch_shapes=[
                pltpu.VMEM((ROWS, KEY_W), BF16),
                pltpu.VMEM((stage, KEY_W), BF16),
                pltpu.VMEM((stage, KEY_W), BF16),
                pltpu.VMEM((stage, VAL_W), BF16),
                pltpu.VMEM((stage, VAL_W), BF16),
                pltpu.VMEM((ROWS, stage), F32),
                pltpu.VMEM((ROWS, stage), F32),
                pltpu.VMEM((ROWS, 1), F32),
                pltpu.VMEM((ROWS, VAL_W), F32),
            ],
        ),
        out_shape=[out, out],
        compiler_params=_cparams(("parallel", "parallel", "arbitrary")),
        name="nsa_attend",
    )(lists, counts, qfeat, q, selbias, ks, vs, kw, vw)


def _outproj1_body(h_ref, oc_ref, os_ref, ow_ref, gt_ref, ex_ref, wo_ref, o_ref):
    gt = gt_ref[...]
    hi = gt.astype(BF16)
    lo = (gt - hi.astype(F32)).astype(BF16)
    o = jnp.zeros(oc_ref.shape, F32)
    for c, br in enumerate((oc_ref, os_ref, ow_ref)):
        ge = _dot(hi, ex_ref[c]) + _dot(lo, ex_ref[c])
        o = o + ge * br[...]
    o_ref[...] = h_ref[...] + _dot(o.astype(BF16), wo_ref[...])


def _outproj1(h, oc, os_, ow, gates, expand, w_out, *, tm=512):
    m = h.shape[0]
    row = pl.BlockSpec((tm, D_MODEL), lambda i: (i, 0))
    return pl.pallas_call(
        _outproj1_body,
        grid=(m // tm,),
        in_specs=[row, row, row, row,
                  pl.BlockSpec((tm, LANE), lambda i: (i, 0)),
                  pl.BlockSpec((3, LANE, D_MODEL), lambda i: (0, 0, 0)),
                  pl.BlockSpec((D_MODEL, D_MODEL), lambda i: (0, 0))],
        out_specs=row,
        out_shape=jax.ShapeDtypeStruct((m, D_MODEL), F32),
        compiler_params=_cparams(("parallel",)),
        name="outproj1",
    )(h, oc, os_, ow, gates, expand, w_out)


def _pad_heads(w, heads, width, padded):
    lead = w.shape[:-1]
    w = w.reshape(lead + (heads, width))
    w = jnp.pad(w, [(0, 0)] * len(lead) + [(0, 0), (0, padded - width)])
    return w.reshape(lead + (heads * padded,))


def _dup_groups(w):
    lead = w.shape[:-1]
    w = w.reshape(lead + (NSA_KV_GROUPS, NSA_HDIM))
    return jnp.concatenate([w, w], axis=-1).reshape(lead + (NSA_KV_GROUPS * LANE,))


def _np_split3(x):
    out = []
    r = np.asarray(x, np.float64)
    for _ in range(3):
        part = np.asarray(np.asarray(r, np.float32).astype(jnp.bfloat16), np.float64)
        out.append(part.astype(np.float32))
        r = r - part
    return out


def _position_terms(pos):
    pos = np.asarray(pos)
    f = np.zeros((pos.shape[0], LANE), np.float32)
    f[:, FEAT0 + 0:FEAT0 + 3] = (pos >> SLC_SHIFT)[:, None]
    f[:, FEAT0 + 3:FEAT0 + 6] = (pos & (SLC_BLOCK - 1))[:, None]
    f[:, FEAT0 + 6:FEAT0 + 9] = 1.0
    return f


def _slope_terms(nqt):
    slopes = np.array([2.0 ** (-8.0 * (i + 1) / NSA_HEADS) for i in range(NSA_HEADS)], dtype=np.float32)
    slopes = slopes.reshape(NSA_HPG, NSA_KV_GROUPS).T.astype(np.float64)
    f = np.zeros((NSA_KV_GROUPS, nqt, NSA_HPG, LANE), np.float32)
    t0 = (np.arange(nqt) * Q_BLOCK).astype(np.float64)
    for k, part in enumerate(_np_split3(slopes * SLC_BLOCK)):
        f[:, :, :, FEAT0 + k] = part[:, None, :]
    for k, part in enumerate(_np_split3(slopes)):
        f[:, :, :, FEAT0 + 3 + k] = part[:, None, :]
    for k, part in enumerate(_np_split3(-slopes[:, None, :] * t0[None, :, None])):
        f[:, :, :, FEAT0 + 6 + k] = part
    return f


def _key_terms(t):
    pos = np.arange(t)
    onehot = np.zeros((t, MAX_SLC), np.float32)
    onehot[pos, pos >> SLC_SHIFT] = 1.0
    return np.concatenate([_position_terms(pos), onehot], axis=1)


def _cmp_to_slc_t(ncmp_pad):
    cs = np.arange(ncmp_pad) * CMP_STRIDE
    ss = np.arange(MAX_SLC) * SLC_BLOCK
    ov = np.minimum(cs[None] + CMP_BLOCK, ss[:, None] + SLC_BLOCK) - np.maximum(cs[None], ss[:, None])
    return (np.clip(ov, 0, None) / CMP_BLOCK).astype(np.float32)


def _gate_expand():
    e = np.zeros((3, LANE, NSA_HEADS * NSA_HDIM), np.float32)
    for c in range(3):
        for g in range(NSA_KV_GROUPS):
            for h in range(NSA_HPG):
                col = g * NSA_HPG * 3 + h * 3 + c
                base = (g * NSA_HPG + h) * NSA_HDIM
                e[c, col, base:base + NSA_HDIM] = 1.0
    return e


def _compress_weights(w1, w2, pe, dup):
    half = CMP_BLOCK // 2
    eye = jnp.eye(NSA_KV_GROUPS, dtype=w1.dtype)
    w1 = w1.reshape(2, half, NSA_HDIM, CMP_HIDDEN)
    wexp = jnp.einsum('sjdc,gh->sjgdhc', w1, eye).reshape(2, half * NSA_KV, NSA_KV_GROUPS * CMP_HIDDEN)
    pe = jnp.broadcast_to(pe.reshape(2, half, 1, NSA_HDIM), (2, half, NSA_KV_GROUPS, NSA_HDIM)).reshape(2, 1, half * NSA_KV)
    w2w = jnp.concatenate([w2, w2 if dup else jnp.zeros_like(w2)], axis=-1)
    return pe[0], pe[1], wexp[0].astype(BF16), wexp[1].astype(BF16), w2w.astype(BF16)


def _layer0_mixer(h3, norm_mix, a_w_in, a_gate_w2, a_gate_b, a_gla_norm, a_pool_w, a_pool_scale, a_w_out):
    bsz, t, d = h3.shape
    cuts = np.cumsum([GLA_KEY, GLA_KEY, GLA_VAL, GLA_VAL, GLA_GATE_RANK]).tolist()
    wq, wk, wv, wg, wgr, wp = jnp.split(a_w_in, cuts, axis=-1)
    w0 = jnp.concatenate([
        _pad_heads(wq, GLA_HEADS, GLA_DK, GLA_DK_PAD), _pad_heads(wk, GLA_HEADS, GLA_DK, GLA_DK_PAD),
        wv, wg, wp, jnp.pad(wgr, ((0, 0), (0, LANE - GLA_GATE_RANK)))], axis=-1).astype(BF16)
    w2 = jnp.pad(_pad_heads(a_gate_w2, GLA_HEADS, GLA_DK, GLA_DK_PAD), ((0, LANE - GLA_GATE_RANK), (0, 0))).astype(BF16)
    b2 = _pad_heads(a_gate_b, GLA_HEADS, GLA_DK, GLA_DK_PAD).reshape(1, -1)
    q, k, v, g, p, la = _inproj0(h3.reshape(bsz * t, d), norm_mix.reshape(1, d), w0, w2, b2)
    r3 = lambda z: z.reshape(bsz, t, -1)
    oa = _gla(r3(q), r3(k), r3(v), r3(g), r3(la), a_gla_norm.reshape(1, GLA_DV))
    return _outproj0(h3, oa, r3(p), a_pool_w.astype(BF16), a_pool_scale.reshape(1, -1), a_w_out.astype(BF16))


def _layer1_mixer(h3, norm_mix, c_w_in, c_cmp_pe, c_cmpk_w1, c_cmpk_w2, c_cmpv_w1, c_cmpv_w2, c_w_out):
    bsz, t, d = h3.shape
    assert t // SLC_BLOCK <= MAX_SLC and t >= WINDOW + Q_BLOCK
    nqt = t // Q_BLOCK
    cuts = (NSA_HEADS * NSA_HDIM + NSA_KV * np.arange(7)).tolist()
    wq, wkc, wvc, wks, wvs, wkw, wvw, wgt = jnp.split(c_w_in, cuts, axis=-1)
    w1 = jnp.concatenate([wq, wkc, wvc, _pad_heads(wks, NSA_KV_GROUPS, NSA_HDIM, LANE), _dup_groups(wvs),
                          _pad_heads(wkw, NSA_KV_GROUPS, NSA_HDIM, LANE), _dup_groups(wvw),
                          jnp.pad(wgt, ((0, 0), (0, LANE - wgt.shape[-1])))], axis=-1).astype(BF16)
    kfeat = jnp.asarray(_key_terms(t)).astype(BF16)
    q, kc, vc, ks, vs, kw, vw, gates = _inproj1(h3.reshape(bsz * t, d), norm_mix.reshape(1, d), w1, kfeat, t=t)
    nrow = t // CMP_STRIDE
    cfeat = jnp.asarray(_position_terms(np.arange(nrow) * CMP_STRIDE + CMP_BLOCK - 1))
    zero = jnp.zeros_like(cfeat)
    kcmp = _compress(kc.reshape(bsz, nrow, CMP_STRIDE * NSA_KV),
                     *_compress_weights(c_cmpk_w1, c_cmpk_w2, c_cmp_pe, dup=False), cfeat)
    vcmp = _compress(vc.reshape(bsz, nrow, CMP_STRIDE * NSA_KV),
                     *_compress_weights(c_cmpv_w1, c_cmpv_w2, c_cmp_pe, dup=True), zero)
    r3 = lambda z: z.reshape(bsz, t, -1)
    qfeat = jnp.asarray(_slope_terms(nqt))
    mcst = jnp.asarray(_cmp_to_slc_t(nrow)).astype(BF16)
    oc, selbias, lists = _nsa_select(qfeat, r3(q), kcmp, vcmp, mcst)
    os_, ow = _nsa_attend(lists[:, :, :, 0, :].reshape(-1), lists[:, :, :, 1, 0].reshape(-1), qfeat, r3(q), selbias,
                          r3(ks), r3(vs), r3(kw), r3(vw))
    f2 = lambda z: z.reshape(bsz * t, -1)
    out = _outproj1(h3.reshape(bsz * t, d), f2(oc), f2(os_), f2(ow), gates,
                    jnp.asarray(_gate_expand()).astype(BF16), c_w_out.astype(BF16))
    return out.reshape(bsz, t, d)


def kernel(x, norm_ffn1, ffn1_wg, ffn1_wu, ffn1_wd, norm_mix, norm_ffn2, ffn2_wg, ffn2_wu, ffn2_wd, a_w_in, a_gate_w2, a_gate_b, a_gla_norm, a_pool_w, a_pool_scale, a_w_out, c_w_in, c_cmp_pe, c_cmpk_w1, c_cmpk_w2, c_cmpv_w1, c_cmpv_w2, c_w_out, final_norm):
    bsz, t, d = x.shape
    depth = norm_ffn1.shape[0]
    fg = final_norm.reshape(1, d)
    h = x
    for l in range(depth):
        h = _ffn(h.reshape(bsz * t, d), norm_ffn1[l].reshape(1, d), ffn1_wg[l].astype(BF16), ffn1_wu[l].astype(BF16),
                 ffn1_wd[l].astype(BF16), fg, final_norm=False).reshape(bsz, t, d)
        i = l // 2
        if l % 2 == 0:
            h = _layer0_mixer(h, norm_mix[l], a_w_in[i], a_gate_w2[i], a_gate_b[i], a_gla_norm[i],
                              a_pool_w[i], a_pool_scale[i], a_w_out[i])
        else:
            h = _layer1_mixer(h, norm_mix[l], c_w_in[i], c_cmp_pe[i], c_cmpk_w1[i], c_cmpk_w2[i],
                              c_cmpv_w1[i], c_cmpv_w2[i], c_w_out[i])
        h = _ffn(h.reshape(bsz * t, d), norm_ffn2[l].reshape(1, d), ffn2_wg[l].astype(BF16), ffn2_wu[l].astype(BF16),
                 ffn2_wd[l].astype(BF16), fg, final_norm=(l == depth - 1)).reshape(bsz, t, d)
    return h
```

```python
import functools
import math

import numpy as np
import jax
import jax.numpy as jnp
from jax import lax
from jax.experimental import pallas as pl
from jax.experimental.pallas import tpu as pltpu

F32 = jnp.float32
BF16 = jnp.bfloat16

D_MODEL = 1024
D_FF = 2816
EPS = 1e-6

GLA_HEADS = 4
GLA_DV = 128
GLA_DK = 64
GLA_DK_PAD = 128
GLA_KEY = GLA_HEADS * GLA_DK
GLA_VAL = GLA_HEADS * GLA_DV
GLA_GATE_RANK = 16
GLA_TAU = 16.0
GLA_CHUNK = 64

POOL_WINDOWS = (2, 4, 8, 16)
POOL_GROUPS = 4
POOL_GDIM = 128
POOL_WIDTH = POOL_GROUPS * POOL_GDIM
POOL_HALO = 16

NSA_HDIM = 64
NSA_HEADS = 16
NSA_KV_GROUPS = 2
NSA_HPG = NSA_HEADS // NSA_KV_GROUPS
NSA_PAIRS = NSA_HPG // 2
NSA_KV = NSA_KV_GROUPS * NSA_HDIM
CMP_BLOCK = 32
CMP_STRIDE = 16
CMP_HIDDEN = 256
SLC_BLOCK = 64
SLC_SHIFT = 6
SLC_TOP = 16
WINDOW = 512
Q_BLOCK = 128
NEG = -1e30
FORCE = 1e4

LANE = 128
VMEM_LIMIT = 56 * 1024 * 1024

MAX_SLC = LANE
FEAT0 = NSA_HDIM
STAGE_BLOCKS = 8
ROWS = NSA_HPG * Q_BLOCK


def _cparams(sem):
    return pltpu.CompilerParams(dimension_semantics=sem, vmem_limit_bytes=VMEM_LIMIT)


def _rmsnorm(x, g):
    return x * lax.rsqrt(jnp.mean(x * x, axis=-1, keepdims=True) + EPS) * g


def _dot(a, b):
    return jnp.dot(a, b, preferred_element_type=F32)


def _dot_nt(a, b):
    return lax.dot_general(a, b, (((1,), (1,)), ((), ())), preferred_element_type=F32)


def _split3(x):
    hi = x.astype(BF16)
    r1 = x - hi.astype(F32)
    mid = r1.astype(BF16)
    lo = (r1 - mid.astype(F32)).astype(BF16)
    return hi, mid, lo


def _ffn_body(h_ref, g_ref, wg_ref, wu_ref, wd_ref, fg_ref, o_ref, xn_ref, acc_ref, *, final_norm):
    j = pl.program_id(1)

    @pl.when(j == 0)
    def _():
        xn_ref[...] = _rmsnorm(h_ref[...], g_ref[...]).astype(BF16)
        acc_ref[...] = jnp.zeros_like(acc_ref)

    xn = xn_ref[...]
    a = _dot(xn, wg_ref[...])
    u = _dot(xn, wu_ref[...])
    hid = (a * jax.nn.sigmoid(a) * u).astype(BF16)
    acc_ref[...] += _dot(hid, wd_ref[...])

    @pl.when(j == pl.num_programs(1) - 1)
    def _():
        y = h_ref[...] + 0.5 * acc_ref[...]
        if final_norm:
            y = _rmsnorm(y, fg_ref[...])
        o_ref[...] = y


def _ffn(h, gain, wg, wu, wd, final_gain, *, final_norm, tm=1024, tf=256):
    m = h.shape[0]
    assert m % tm == 0 and D_FF % tf == 0
    return pl.pallas_call(
        functools.partial(_ffn_body, final_norm=final_norm),
        grid=(m // tm, D_FF // tf),
        in_specs=[
            pl.BlockSpec((tm, D_MODEL), lambda i, j: (i, 0)),
            pl.BlockSpec((1, D_MODEL), lambda i, j: (0, 0)),
            pl.BlockSpec((D_MODEL, tf), lambda i, j: (0, j)),
            pl.BlockSpec((D_MODEL, tf), lambda i, j: (0, j)),
            pl.BlockSpec((tf, D_MODEL), lambda i, j: (j, 0)),
            pl.BlockSpec((1, D_MODEL), lambda i, j: (0, 0)),
        ],
        out_specs=pl.BlockSpec((tm, D_MODEL), lambda i, j: (i, 0)),
        out_shape=jax.ShapeDtypeStruct((m, D_MODEL), F32),
        scratch_shapes=[pltpu.VMEM((tm, D_MODEL), BF16), pltpu.VMEM((tm, D_MODEL), F32)],
        compiler_params=_cparams(("parallel", "arbitrary")),
        name="ffn",
    )(h, gain, wg, wu, wd, final_gain)


W0_Q, W0_K, W0_V, W0_G, W0_P, W0_GR = 0, 512, 1024, 1536, 2048, 2560
W0_COLS = 2688


def _inproj0_body(h_ref, g_ref, w_ref, w2_ref, b2_ref, q_ref, k_ref, v_ref, gg_ref, p_ref, la_ref):
    xn = _rmsnorm(h_ref[...], g_ref[...]).astype(BF16)
    q_ref[...] = _dot(xn, w_ref[:, W0_Q:W0_Q + 512])
    k_ref[...] = _dot(xn, w_ref[:, W0_K:W0_K + 512])
    v_ref[...] = _dot(xn, w_ref[:, W0_V:W0_V + 512])
    gg_ref[...] = _dot(xn, w_ref[:, W0_G:W0_G + 512])
    p_ref[...] = _dot(xn, w_ref[:, W0_P:W0_P + 512])
    gr = _dot(xn, w_ref[:, W0_GR:W0_GR + LANE]).astype(BF16)
    z = _dot(gr, w2_ref[...]) + b2_ref[...]
    log_sig = jnp.minimum(z, 0.0) - jnp.log1p(jnp.exp(-jnp.abs(z)))
    la_ref[...] = log_sig / GLA_TAU


def _inproj0(h, gain, w0, w2, b2, *, tm=512):
    m = h.shape[0]
    wide = pl.BlockSpec((tm, 512), lambda i: (i, 0))
    return pl.pallas_call(
        _inproj0_body,
        grid=(m // tm,),
        in_specs=[
            pl.BlockSpec((tm, D_MODEL), lambda i: (i, 0)),
            pl.BlockSpec((1, D_MODEL), lambda i: (0, 0)),
            pl.BlockSpec((D_MODEL, W0_COLS), lambda i: (0, 0)),
            pl.BlockSpec((LANE, 512), lambda i: (0, 0)),
            pl.BlockSpec((1, 512), lambda i: (0, 0)),
        ],
        out_specs=[wide] * 6,
        out_shape=[jax.ShapeDtypeStruct((m, 512), F32)] * 6,
        compiler_params=_cparams(("parallel",)),
        name="inproj0",
    )(h, gain, w0, w2, b2)


def _gla_body(q_ref, k_ref, v_ref, g_ref, la_ref, gn_ref, o_ref, st_ref, *, tt):
    c = GLA_CHUNK
    nchunk = tt // c

    @pl.when(pl.program_id(2) == 0)
    def _():
        st_ref[...] = jnp.zeros_like(st_ref)

    la = la_ref[0]
    row = lax.broadcasted_iota(jnp.int32, (tt, tt), 0)
    col = lax.broadcasted_iota(jnp.int32, (tt, tt), 1)
    shift = int(math.log2(c))
    same_chunk = (row >> shift) == (col >> shift)
    cum_m = jnp.where(same_chunk & (col <= row), 1.0, 0.0).astype(BF16)
    tot_m = jnp.where(same_chunk, 1.0, 0.0).astype(BF16)
    la3 = _split3(la)
    b = sum(_dot(cum_m, t) for t in la3)
    b_last = sum(_dot(tot_m, t) for t in la3)

    q = q_ref[0] * (GLA_DK ** -0.5)
    k = k_ref[0]
    q_dec = (q * jnp.exp(b)).astype(BF16)
    k_inv = (k * jnp.exp(-b)).astype(BF16)
    k_end = (k * jnp.exp(b_last - b)).astype(BF16)
    decay = jnp.exp(b_last)
    v = v_ref[0]
    v_bf = v.astype(BF16)

    ci = lax.broadcasted_iota(jnp.int32, (c, c), 0)
    cj = lax.broadcasted_iota(jnp.int32, (c, c), 1)
    causal = cj <= ci

    st = st_ref[...]
    outs = []
    for n in range(nchunk):
        sl = slice(n * c, (n + 1) * c)
        att = jnp.where(causal, _dot_nt(q_dec[sl], k_inv[sl]), 0.0)
        o = _dot(att.astype(BF16), v_bf[sl]) + _dot_nt(q_dec[sl], st.astype(BF16))
        outs.append(o)
        d_state = _dot(v[sl].T.astype(BF16), k_end[sl])
        st = decay[n * c:n * c + 1, :] * st + d_state
    st_ref[...] = st

    o = jnp.concatenate(outs, axis=0)
    o = o * lax.rsqrt(jnp.mean(o * o, axis=-1, keepdims=True) + EPS) * gn_ref[...]
    g = g_ref[0]
    o_ref[0] = o * (g * jax.nn.sigmoid(g))


def _gla(q, k, v, g, la, gn, *, tt=512):
    bsz, t, _ = q.shape
    blk = pl.BlockSpec((1, tt, LANE), lambda b, h, i: (b, i, h))
    return pl.pallas_call(
        functools.partial(_gla_body, tt=tt),
        grid=(bsz, GLA_HEADS, t // tt),
        in_specs=[blk, blk, blk, blk, blk, pl.BlockSpec((1, LANE), lambda b, h, i: (0, 0))],
        out_specs=blk,
        out_shape=jax.ShapeDtypeStruct((bsz, t, GLA_VAL), F32),
        scratch_shapes=[pltpu.VMEM((GLA_DV, GLA_DK_PAD), F32)],
        compiler_params=_cparams(("parallel", "parallel", "arbitrary")),
        name="gla",
    )(q, k, v, g, la, gn)


def _outproj0_body(h_ref, oa_ref, p_ref, pw_ref, ps_ref, wo_ref, o_ref, pbuf_ref, *, tm):
    i = pl.program_id(1)

    @pl.when(i == 0)
    def _():
        pbuf_ref[0:POOL_HALO, :] = jnp.zeros((POOL_HALO, POOL_WIDTH), F32)

    p = p_ref[0]
    pbuf_ref[POOL_HALO:POOL_HALO + tm, :] = p
    tpos = i * tm + lax.broadcasted_iota(jnp.int32, (tm, 1), 0)
    mix = _dot(oa_ref[0].astype(BF16), wo_ref[0:GLA_VAL, :])
    for gi, w in enumerate(POOL_WINDOWS):
        ls = slice(gi * POOL_GDIM, (gi + 1) * POOL_GDIM)
        acc = p[:, ls]
        for s in range(1, w):
            acc = acc + pbuf_ref[POOL_HALO - s:POOL_HALO - s + tm, ls]
        cnt = jnp.minimum(tpos + 1, w).astype(F32)
        pooled = acc / cnt - p[:, ls]
        ob = _dot(pooled.astype(BF16), pw_ref[gi]) * ps_ref[:, ls]
        mix = mix + _dot(ob.astype(BF16), wo_ref[GLA_VAL + gi * POOL_GDIM:GLA_VAL + (gi + 1) * POOL_GDIM, :])
    pbuf_ref[0:POOL_HALO, :] = p[tm - POOL_HALO:, :]
    o_ref[0] = h_ref[0] + mix


def _outproj0(h, oa, p, pool_w, pool_scale, w_out, *, tm=512):
    bsz, t, _ = h.shape
    return pl.pallas_call(
        functools.partial(_outproj0_body, tm=tm),
        grid=(bsz, t // tm),
        in_specs=[
            pl.BlockSpec((1, tm, D_MODEL), lambda b, i: (b, i, 0)),
            pl.BlockSpec((1, tm, GLA_VAL), lambda b, i: (b, i, 0)),
            pl.BlockSpec((1, tm, POOL_WIDTH), lambda b, i: (b, i, 0)),
            pl.BlockSpec((POOL_GROUPS, POOL_GDIM, POOL_GDIM), lambda b, i: (0, 0, 0)),
            pl.BlockSpec((1, POOL_WIDTH), lambda b, i: (0, 0)),
            pl.BlockSpec((D_MODEL, D_MODEL), lambda b, i: (0, 0)),
        ],
        out_specs=pl.BlockSpec((1, tm, D_MODEL), lambda b, i: (b, i, 0)),
        out_shape=jax.ShapeDtypeStruct((bsz, t, D_MODEL), F32),
        scratch_shapes=[pltpu.VMEM((POOL_HALO + tm, POOL_WIDTH), F32)],
        compiler_params=_cparams(("parallel", "arbitrary")),
        name="outproj0",
    )(h, oa, p, pool_w, pool_scale, w_out)


KEY_W = 2 * LANE
VAL_W = 2 * LANE
W1_Q, W1_KC, W1_VC, W1_KS, W1_VS, W1_KW, W1_VW, W1_GT = 0, 1024, 1152, 1280, 1536, 1792, 2048, 2304
W1_COLS = 2432


def _inproj1_body(h_ref, g_ref, w_ref, kf_ref, q_ref, kc_ref, vc_ref, ks_ref, vs_ref, kw_ref, vw_ref, gt_ref):
    xn = _rmsnorm(h_ref[...], g_ref[...]).astype(BF16)
    tm = xn.shape[0]
    q_ref[...] = (_dot(xn, w_ref[:, W1_Q:W1_Q + 1024]) * (NSA_HDIM ** -0.5)).astype(BF16)
    kc_ref[...] = _dot(xn, w_ref[:, W1_KC:W1_KC + NSA_KV])
    vc_ref[...] = _dot(xn, w_ref[:, W1_VC:W1_VC + NSA_KV])
    feat = kf_ref[:, 0:LANE].astype(F32)
    onehot = kf_ref[:, LANE:2 * LANE]
    ones = jnp.ones((tm, LANE), BF16)
    for g in range(NSA_KV_GROUPS):
        gl = slice(g * LANE, (g + 1) * LANE)
        ks = (_dot(xn, w_ref[:, W1_KS + g * LANE:W1_KS + (g + 1) * LANE]) + feat).astype(BF16)
        ks_ref[:, g * KEY_W:g * KEY_W + LANE] = ks
        ks_ref[:, g * KEY_W + LANE:(g + 1) * KEY_W] = onehot
        kw_ref[:, gl] = (_dot(xn, w_ref[:, W1_KW + g * LANE:W1_KW + (g + 1) * LANE]) + feat).astype(BF16)
        vs_ref[:, g * VAL_W:g * VAL_W + LANE] = _dot(xn, w_ref[:, W1_VS + g * LANE:W1_VS + (g + 1) * LANE]).astype(BF16)
        vs_ref[:, g * VAL_W + LANE:(g + 1) * VAL_W] = ones
        vw_ref[:, g * VAL_W:g * VAL_W + LANE] = _dot(xn, w_ref[:, W1_VW + g * LANE:W1_VW + (g + 1) * LANE]).astype(BF16)
        vw_ref[:, g * VAL_W + LANE:(g + 1) * VAL_W] = ones
    gt_ref[...] = jax.nn.sigmoid(_dot(xn, w_ref[:, W1_GT:W1_GT + LANE]))


def _inproj1(h, gain, w1, kfeat, *, t, tm=512):
    m = h.shape[0]
    per_seq = t // tm
    spec = lambda n: pl.BlockSpec((tm, n), lambda i: (i, 0))
    shp = lambda n, dt: jax.ShapeDtypeStruct((m, n), dt)
    g = NSA_KV_GROUPS
    return pl.pallas_call(
        _inproj1_body,
        grid=(m // tm,),
        in_specs=[
            pl.BlockSpec((tm, D_MODEL), lambda i: (i, 0)),
            pl.BlockSpec((1, D_MODEL), lambda i: (0, 0)),
            pl.BlockSpec((D_MODEL, W1_COLS), lambda i: (0, 0)),
            pl.BlockSpec((tm, KEY_W), lambda i: (i % per_seq, 0)),
        ],
        out_specs=[spec(1024), spec(NSA_KV), spec(NSA_KV), spec(g * KEY_W), spec(g * VAL_W), spec(g * LANE),
                   spec(g * VAL_W), spec(LANE)],
        out_shape=[shp(1024, BF16), shp(NSA_KV, F32), shp(NSA_KV, F32), shp(g * KEY_W, BF16), shp(g * VAL_W, BF16),
                   shp(g * LANE, BF16), shp(g * VAL_W, BF16), shp(LANE, F32)],
        compiler_params=_cparams(("parallel",)),
        name="inproj1",
    )(h, gain, w1, kfeat)


def _compress_body(z_ref, pea_ref, peb_ref, wa_ref, wb_ref, w2_ref, cf_ref, o_ref, sh_ref, *, nrow):
    z = z_ref[0]
    a = _dot((z + pea_ref[...]).astype(BF16), wa_ref[...])
    bm = _dot((z + peb_ref[...]).astype(BF16), wb_ref[...])
    sh_ref[0:nrow, :] = bm
    sh_ref[nrow:nrow + 8, :] = jnp.zeros((8, sh_ref.shape[1]), F32)
    x = a + sh_ref[1:nrow + 1, :]
    cdf = 0.5 * (1.0 + jnp.tanh(math.sqrt(2.0 / math.pi) * (x + 0.044715 * (x * x * x))))
    hid = (x * cdf).astype(BF16)
    for g in range(NSA_KV_GROUPS):
        o_ref[0, g] = (_dot(hid[:, g * CMP_HIDDEN:(g + 1) * CMP_HIDDEN], w2_ref[...]) + cf_ref[...]).astype(BF16)


def _compress(z, pea, peb, wa, wb, w2, cfeat):
    bsz, nrow, zw = z.shape
    hw = NSA_KV_GROUPS * CMP_HIDDEN
    return pl.pallas_call(
        functools.partial(_compress_body, nrow=nrow),
        grid=(bsz,),
        in_specs=[
            pl.BlockSpec((1, nrow, zw), lambda b: (b, 0, 0)),
            pl.BlockSpec((1, zw), lambda b: (0, 0)),
            pl.BlockSpec((1, zw), lambda b: (0, 0)),
            pl.BlockSpec((zw, hw), lambda b: (0, 0)),
            pl.BlockSpec((zw, hw), lambda b: (0, 0)),
            pl.BlockSpec((CMP_HIDDEN, LANE), lambda b: (0, 0)),
            pl.BlockSpec((nrow, LANE), lambda b: (0, 0)),
        ],
        out_specs=pl.BlockSpec((1, NSA_KV_GROUPS, nrow, LANE), lambda b: (b, 0, 0, 0)),
        out_shape=jax.ShapeDtypeStruct((bsz, NSA_KV_GROUPS, nrow, LANE), BF16),
        scratch_shapes=[pltpu.VMEM((nrow + 8, hw), F32)],
        compiler_params=_cparams(("parallel",)),
        name="compress",
    )(z, pea, peb, wa, wb, w2, cfeat)


def _fill_query_rows(qa_ref, q_ref, qf_ref, u=0):
    lane_lo = lax.broadcasted_iota(jnp.int32, (Q_BLOCK, LANE), 1) < NSA_HDIM
    for pr in range(NSA_PAIRS):
        even = q_ref[0, u * Q_BLOCK:(u + 1) * Q_BLOCK, pr * LANE:(pr + 1) * LANE]
        odd = pltpu.roll(even.astype(F32), NSA_HDIM, axis=1).astype(BF16)
        for hh, src in enumerate((even, odd)):
            h = 2 * pr + hh
            feat = jnp.broadcast_to(qf_ref[0, u, h:h + 1, :], (Q_BLOCK, LANE)).astype(BF16)
            qa_ref[h * Q_BLOCK:(h + 1) * Q_BLOCK, 0:LANE] = jnp.where(lane_lo, src, feat)


def _merge_pairs(o_ref, o, u=0):
    lane_lo = lax.broadcasted_iota(jnp.int32, (Q_BLOCK, LANE), 1) < NSA_HDIM
    for pr in range(NSA_PAIRS):
        a = o[(2 * pr) * Q_BLOCK:(2 * pr + 1) * Q_BLOCK]
        b = o[(2 * pr + 1) * Q_BLOCK:(2 * pr + 2) * Q_BLOCK]
        o_ref[0, u * Q_BLOCK:(u + 1) * Q_BLOCK, pr * LANE:(pr + 1) * LANE] = jnp.where(lane_lo, a, b)


def _head_tile(bias):
    return jnp.concatenate([bias] * NSA_HPG, axis=0)


SEL_TILES = 4


def _compressed_branch(u, qf_ref, q_ref, kc_ref, vc_ref, mcst_ref, oc_ref, qa_ref):
    t0 = (pl.program_id(2) * SEL_TILES + u) * Q_BLOCK
    ncmp_pad = kc_ref.shape[2]
    nq = Q_BLOCK
    _fill_query_rows(qa_ref, q_ref, qf_ref, u)
    tpos = t0 + lax.broadcasted_iota(jnp.int32, (nq, 1), 0)
    cmp_end = lax.broadcasted_iota(jnp.int32, (1, ncmp_pad), 1) * CMP_STRIDE + (CMP_BLOCK - 1)
    bias_c = jnp.where(tpos >= cmp_end, 0.0, NEG)
    any_c = jnp.where(tpos >= (CMP_BLOCK - 1), 1.0, 0.0)
    s = _dot_nt(qa_ref[...], kc_ref[0, 0]) + _head_tile(bias_c)
    e = jnp.exp(s - jnp.max(s, axis=-1, keepdims=True))
    p = e * (_head_tile(any_c) / jnp.sum(e, axis=-1, keepdims=True))
    _merge_pairs(oc_ref, _dot(p.astype(BF16), vc_ref[0, 0]), u)
    p_sum = p[0:nq]
    for h in range(1, NSA_HPG):
        p_sum = p_sum + p[h * nq:(h + 1) * nq]

    return sum(_dot_nt(mcst_ref[...], term) for term in _split3(p_sum))


def _nsa_select_body(qf_ref, q_ref, kc_ref, vc_ref, mcst_ref, oc_ref, sb_ref, lst_ref, *qa_refs):
    nt = SEL_TILES
    nq = Q_BLOCK
    wide = nt * nq
    imp = jnp.concatenate([_compressed_branch(u, qf_ref, q_ref, kc_ref, vc_ref, mcst_ref, oc_ref, qa_refs[u])
                           for u in range(nt)], axis=1)

    blk = lax.broadcasted_iota(jnp.int32, (MAX_SLC, wide), 0)
    tpos_l = pl.program_id(2) * wide + lax.broadcasted_iota(jnp.int32, (MAX_SLC, wide), 1)
    cur = tpos_l >> SLC_SHIFT
    forced = (blk == 0) | (blk == cur) | (blk == cur - 1)
    valid_b = blk * SLC_BLOCK <= tpos_l
    score = jnp.where(valid_b, jnp.where(forced, FORCE, imp), -1.0)
    blk_f = blk.astype(F32)
    sel = jnp.zeros((MAX_SLC, wide), F32)
    for _ in range(SLC_TOP):
        mx = jnp.max(score, axis=0, keepdims=True)
        first = jnp.min(jnp.where(score == mx, blk_f, float(MAX_SLC)), axis=0, keepdims=True)
        pick = blk_f == first
        sel = jnp.where(pick, 1.0, sel)
        score = jnp.where(pick, -2.0, score)
    own = (tpos_l >> (SLC_SHIFT + 1)) << 1
    sel = jnp.where(valid_b & (blk < own), sel, 0.0)
    for u in range(nt):
        sel_u = sel[:, u * nq:(u + 1) * nq]
        sb_ref[0, 0, u * nq:(u + 1) * nq, :] = jnp.where(sel_u.T > 0.5, 0.0, NEG).astype(BF16)

    used_b = jnp.concatenate(
        [jnp.broadcast_to(jnp.max(sel[:, u * nq:(u + 1) * nq], axis=1, keepdims=True), (MAX_SLC, LANE))
         for u in range(nt)], axis=1).astype(BF16)
    r = lax.broadcasted_iota(jnp.int32, (MAX_SLC, LANE), 0)
    c = lax.broadcasted_iota(jnp.int32, (MAX_SLC, LANE), 1)
    before = jnp.where(c < r, 1.0, 0.0).astype(BF16)
    pos = _dot(before, used_b)
    slot_w = (lax.broadcasted_iota(jnp.int32, (MAX_SLC, nt * LANE), 1) & (LANE - 1)).astype(F32)
    place = jnp.where((pos == slot_w) & (used_b > 0.5), 1.0, 0.0).astype(BF16)
    ids = lax.broadcasted_iota(jnp.int32, (8, LANE), 1).astype(F32).astype(BF16)
    lst = _dot(ids, place)
    cnt = _dot(jnp.ones((8, LANE), BF16), used_b)
    lane8 = lax.broadcasted_iota(jnp.int32, (8, nt * LANE), 1)
    filler = ((pl.program_id(2) * nt + (lane8 >> 7)) << 1).astype(F32)
    lst = jnp.where((lane8 & (LANE - 1)).astype(F32) < cnt, lst, filler)
    row0 = lax.broadcasted_iota(jnp.int32, (8, nt * LANE), 0) == 0
    out = jnp.where(row0, lst, cnt).astype(jnp.int32)
    for u in range(nt):
        lst_ref[0, 0, u] = out[:, u * LANE:(u + 1) * LANE]


def _nsa_select(qfeat, q, kcmp, vcmp, mcst):
    bsz, t, _ = q.shape
    ncmp_pad = kcmp.shape[2]
    nqt = t // Q_BLOCK
    gw = NSA_HPG * NSA_HDIM
    assert nqt % SEL_TILES == 0
    nt = SEL_TILES
    qspec = pl.BlockSpec((1, nt * Q_BLOCK, gw), lambda b, g, i: (b, i, g))
    cspec = pl.BlockSpec((1, 1, ncmp_pad, LANE), lambda b, g, i: (b, g, 0, 0))
    return pl.pallas_call(
        _nsa_select_body,
        grid=(bsz, NSA_KV_GROUPS, nqt // nt),
        in_specs=[
            pl.BlockSpec((1, nt, NSA_HPG, LANE), lambda b, g, i: (g, i, 0, 0)),
            qspec, cspec, cspec,
            pl.BlockSpec((MAX_SLC, ncmp_pad), lambda b, g, i: (0, 0)),
        ],
        out_specs=[qspec,
                   pl.BlockSpec((1, 1, nt * Q_BLOCK, MAX_SLC), lambda b, g, i: (b, g, i, 0)),
                   pl.BlockSpec((1, 1, nt, 8, LANE), lambda b, g, i: (b, g, i, 0, 0))],
        out_shape=[jax.ShapeDtypeStruct((bsz, t, NSA_HEADS * NSA_HDIM), F32),
                   jax.ShapeDtypeStruct((bsz, NSA_KV_GROUPS, t, MAX_SLC), BF16),
                   jax.ShapeDtypeStruct((bsz, NSA_KV_GROUPS, nqt, 8, LANE), jnp.int32)],
        scratch_shapes=[pltpu.VMEM((ROWS, LANE), BF16)] * nt,
        compiler_params=_cparams(("parallel", "parallel", "parallel")),
        name="nsa_select",
    )(qfeat, q, kcmp, vcmp, mcst)


def _nsa_attend_body(lst_ref, cnt_ref, qf_ref, q_ref, sb_ref, ks_ref, vs_ref, kw_ref, vw_ref, os_ref, ow_ref,
                     qa_ref, kst0_ref, kst1_ref, vst0_ref, vst1_ref, s0_ref, s1_ref, m_ref, acc_ref):
    b = pl.program_id(0)
    g = pl.program_id(1)
    qi = pl.program_id(2)
    nqt = pl.num_programs(2)
    t0 = pl.multiple_of(qi * Q_BLOCK, Q_BLOCK)
    nq = Q_BLOCK
    _fill_query_rows(qa_ref, q_ref, qf_ref)
    sb = sb_ref[0, 0]
    for h in range(NSA_HPG):
        qa_ref[h * nq:(h + 1) * nq, LANE:2 * LANE] = sb
    qa_lo = qa_ref[:, 0:LANE]

    tile = (b * NSA_KV_GROUPS + g) * nqt + qi
    base = tile * LANE
    count = cnt_ref[tile]

    bufs = ((kst0_ref, vst0_ref, s0_ref), (kst1_ref, vst1_ref, s1_ref))

    def stage_and_score(c, slot):
        kst_ref, vst_ref, s_ref = bufs[slot]
        for j in range(STAGE_BLOCKS):
            idx = jnp.minimum(c * STAGE_BLOCKS + j, LANE - 1)
            r0 = pl.multiple_of(lst_ref[base + idx] * SLC_BLOCK, SLC_BLOCK)
            kst_ref[j * SLC_BLOCK:(j + 1) * SLC_BLOCK, :] = ks_ref[0, pl.ds(r0, SLC_BLOCK), :]
            vst_ref[j * SLC_BLOCK:(j + 1) * SLC_BLOCK, :] = vs_ref[0, pl.ds(r0, SLC_BLOCK), :]
        s_ref[...] = _dot_nt(qa_ref[...], kst_ref[...])

    def softmax_pv(slot):
        _, vst_ref, s_ref = bufs[slot]
        s = s_ref[...]
        m_old = m_ref[...]
        m_new = jnp.maximum(m_old, jnp.max(s, axis=-1, keepdims=True))
        acc_ref[...] = jnp.exp(m_old - m_new) * acc_ref[...] + _dot(jnp.exp(s - m_new).astype(BF16), vst_ref[...])
        m_ref[...] = m_new

    wk = WINDOW + nq
    w0 = pl.multiple_of(jnp.maximum(t0 - WINDOW, 0), nq)
    dist = (t0 + lax.broadcasted_iota(jnp.int32, (nq, 1), 0)) - (w0 + lax.broadcasted_iota(jnp.int32, (1, wk), 1))
    bias_w = jnp.where((dist >= 0) & (dist < WINDOW), 0.0, NEG)
    s = _dot_nt(qa_lo, kw_ref[0, pl.ds(w0, wk), :]) + _head_tile(bias_w)
    e = jnp.exp(s - jnp.max(s, axis=-1, keepdims=True))
    acc = _dot(e.astype(BF16), vw_ref[0, pl.ds(w0, wk), :])
    _merge_pairs(ow_ref, acc[:, 0:LANE] / acc[:, LANE:2 * LANE])

    ti = lax.broadcasted_iota(jnp.int32, (nq, nq), 0)
    tj = lax.broadcasted_iota(jnp.int32, (nq, nq), 1)
    s = _dot_nt(qa_lo, ks_ref[0, pl.ds(t0, nq), 0:LANE]) + _head_tile(jnp.where(tj <= ti, 0.0, NEG))
    m = jnp.max(s, axis=-1, keepdims=True)
    m_ref[...] = m
    acc_ref[...] = _dot(jnp.exp(s - m).astype(BF16), vs_ref[0, pl.ds(t0, nq), :])

    stage_and_score(0, 0)

    n_groups = (count + STAGE_BLOCKS - 1) // STAGE_BLOCKS

    def step(c, carry):
        stage_and_score(2 * c + 1, 1)
        softmax_pv(0)

        @pl.when(2 * c + 1 < n_groups)
        def _():
            stage_and_score(2 * c + 2, 0)
            softmax_pv(1)

        return carry

    lax.fori_loop(0, (n_groups + 1) // 2, step, 0)
    acc = acc_ref[...]
    _merge_pairs(os_ref, acc[:, 0:LANE] / acc[:, LANE:2 * LANE])


def _nsa_attend(lists, counts, qfeat, q, selbias, ks, vs, kw, vw):
    bsz, t, _ = q.shape
    nqt = t // Q_BLOCK
    gw = NSA_HPG * NSA_HDIM
    qspec = pl.BlockSpec((1, Q_BLOCK, gw), lambda b, g, i, lst, cnt: (b, i, g))
    out = jax.ShapeDtypeStruct((bsz, t, NSA_HEADS * NSA_HDIM), F32)
    stage = STAGE_BLOCKS * SLC_BLOCK
    return pl.pallas_call(
        _nsa_attend_body,
        grid_spec=pltpu.PrefetchScalarGridSpec(
            num_scalar_prefetch=2,
            grid=(bsz, NSA_KV_GROUPS, nqt),
            in_specs=[
                pl.BlockSpec((1, 1, NSA_HPG, LANE), lambda b, g, i, lst, cnt: (g, i, 0, 0)),
                qspec,
                pl.BlockSpec((1, 1, Q_BLOCK, MAX_SLC), lambda b, g, i, lst, cnt: (b, g, i, 0)),
                pl.BlockSpec((1, t, KEY_W), lambda b, g, i, lst, cnt: (b, 0, g)),
                pl.BlockSpec((1, t, VAL_W), lambda b, g, i, lst, cnt: (b, 0, g)),
                pl.BlockSpec((1, t, LANE), lambda b, g, i, lst, cnt: (b, 0, g)),
                pl.BlockSpec((1, t, VAL_W), lambda b, g, i, lst, cnt: (b, 0, g)),
            ],
            out_specs=[qspec, qspec],
            scratch_shapes=[
                pltpu.VMEM((ROWS, KEY_W), BF16),
                pltpu.VMEM((stage, KEY_W), BF16),
                pltpu.VMEM((stage, KEY_W), BF16),
                pltpu.VMEM((stage, VAL_W), BF16),
                pltpu.VMEM((stage, VAL_W), BF16),
                pltpu.VMEM((ROWS, stage), F32),
                pltpu.VMEM((ROWS, stage), F32),
                pltpu.VMEM((ROWS, 1), F32),
                pltpu.VMEM((ROWS, VAL_W), F32),
            ],
        ),
        out_shape=[out, out],
        compiler_params=_cparams(("parallel", "parallel", "arbitrary")),
        name="nsa_attend",
    )(lists, counts, qfeat, q, selbias, ks, vs, kw, vw)


def _outproj1_body(h_ref, oc_ref, os_ref, ow_ref, gt_ref, ex_ref, wo_ref, o_ref):
    gt = gt_ref[...]
    hi = gt.astype(BF16)
    lo = (gt - hi.astype(F32)).astype(BF16)
    o = jnp.zeros(oc_ref.shape, F32)
    for c, br in enumerate((oc_ref, os_ref, ow_ref)):
        ge = _dot(hi, ex_ref[c]) + _dot(lo, ex_ref[c])
        o = o + ge * br[...]
    o_ref[...] = h_ref[...] + _dot(o.astype(BF16), wo_ref[...])


def _outproj1(h, oc, os_, ow, gates, expand, w_out, *, tm=512):
    m = h.shape[0]
    row = pl.BlockSpec((tm, D_MODEL), lambda i: (i, 0))
    return pl.pallas_call(
        _outproj1_body,
        grid=(m // tm,),
        in_specs=[row, row, row, row,
                  pl.BlockSpec((tm, LANE), lambda i: (i, 0)),
                  pl.BlockSpec((3, LANE, D_MODEL), lambda i: (0, 0, 0)),
                  pl.BlockSpec((D_MODEL, D_MODEL), lambda i: (0, 0))],
        out_specs=row,
        out_shape=jax.ShapeDtypeStruct((m, D_MODEL), F32),
        compiler_params=_cparams(("parallel",)),
        name="outproj1",
    )(h, oc, os_, ow, gates, expand, w_out)


def _pad_heads(w, heads, width, padded):
    lead = w.shape[:-1]
    w = w.reshape(lead + (heads, width))
    w = jnp.pad(w, [(0, 0)] * len(lead) + [(0, 0), (0, padded - width)])
    return w.reshape(lead + (heads * padded,))


def _dup_groups(w):
    lead = w.shape[:-1]
    w = w.reshape(lead + (NSA_KV_GROUPS, NSA_HDIM))
    return jnp.concatenate([w, w], axis=-1).reshape(lead + (NSA_KV_GROUPS * LANE,))


def _np_split3(x):
    out = []
    r = np.asarray(x, np.float64)
    for _ in range(3):
        part = np.asarray(np.asarray(r, np.float32).astype(jnp.bfloat16), np.float64)
        out.append(part.astype(np.float32))
        r = r - part
    return out


def _position_terms(pos):
    pos = np.asarray(pos)
    f = np.zeros((pos.shape[0], LANE), np.float32)
    f[:, FEAT0 + 0:FEAT0 + 3] = (pos >> SLC_SHIFT)[:, None]
    f[:, FEAT0 + 3:FEAT0 + 6] = (pos & (SLC_BLOCK - 1))[:, None]
    f[:, FEAT0 + 6:FEAT0 + 9] = 1.0
    return f


def _slope_terms(nqt):
    slopes = np.array([2.0 ** (-8.0 * (i + 1) / NSA_HEADS) for i in range(NSA_HEADS)], dtype=np.float32)
    slopes = slopes.reshape(NSA_HPG, NSA_KV_GROUPS).T.astype(np.float64)
    f = np.zeros((NSA_KV_GROUPS, nqt, NSA_HPG, LANE), np.float32)
    t0 = (np.arange(nqt) * Q_BLOCK).astype(np.float64)
    for k, part in enumerate(_np_split3(slopes * SLC_BLOCK)):
        f[:, :, :, FEAT0 + k] = part[:, None, :]
    for k, part in enumerate(_np_split3(slopes)):
        f[:, :, :, FEAT0 + 3 + k] = part[:, None, :]
    for k, part in enumerate(_np_split3(-slopes[:, None, :] * t0[None, :, None])):
        f[:, :, :, FEAT0 + 6 + k] = part
    return f


def _key_terms(t):
    pos = np.arange(t)
    onehot = np.zeros((t, MAX_SLC), np.float32)
    onehot[pos, pos >> SLC_SHIFT] = 1.0
    return np.concatenate([_position_terms(pos), onehot], axis=1)


def _cmp_to_slc_t(ncmp_pad):
    cs = np.arange(ncmp_pad) * CMP_STRIDE
    ss = np.arange(MAX_SLC) * SLC_BLOCK
    ov = np.minimum(cs[None] + CMP_BLOCK, ss[:, None] + SLC_BLOCK) - np.maximum(cs[None], ss[:, None])
    return (np.clip(ov, 0, None) / CMP_BLOCK).astype(np.float32)


def _gate_expand():
    e = np.zeros((3, LANE, NSA_HEADS * NSA_HDIM), np.float32)
    for c in range(3):
        for g in range(NSA_KV_GROUPS):
            for h in range(NSA_HPG):
                col = g * NSA_HPG * 3 + h * 3 + c
                base = (g * NSA_HPG + h) * NSA_HDIM
                e[c, col, base:base + NSA_HDIM] = 1.0
    return e


def _compress_weights(w1, w2, pe, dup):
    half = CMP_BLOCK // 2
    eye = jnp.eye(NSA_KV_GROUPS, dtype=w1.dtype)
    w1 = w1.reshape(2, half, NSA_HDIM, CMP_HIDDEN)
    wexp = jnp.einsum('sjdc,gh->sjgdhc', w1, eye).reshape(2, half * NSA_KV, NSA_KV_GROUPS * CMP_HIDDEN)
    pe = jnp.broadcast_to(pe.reshape(2, half, 1, NSA_HDIM), (2, half, NSA_KV_GROUPS, NSA_HDIM)).reshape(2, 1, half * NSA_KV)
    w2w = jnp.concatenate([w2, w2 if dup else jnp.zeros_like(w2)], axis=-1)
    return pe[0], pe[1], wexp[0].astype(BF16), wexp[1].astype(BF16), w2w.astype(BF16)


def _layer0_mixer(h3, norm_mix, a_w_in, a_gate_w2, a_gate_b, a_gla_norm, a_pool_w, a_pool_scale, a_w_out):
    bsz, t, d = h3.shape
    cuts = np.cumsum([GLA_KEY, GLA_KEY, GLA_VAL, GLA_VAL, GLA_GATE_RANK]).tolist()
    wq, wk, wv, wg, wgr, wp = jnp.split(a_w_in, cuts, axis=-1)
    w0 = jnp.concatenate([
        _pad_heads(wq, GLA_HEADS, GLA_DK, GLA_DK_PAD), _pad_heads(wk, GLA_HEADS, GLA_DK, GLA_DK_PAD),
        wv, wg, wp, jnp.pad(wgr, ((0, 0), (0, LANE - GLA_GATE_RANK)))], axis=-1).astype(BF16)
    w2 = jnp.pad(_pad_heads(a_gate_w2, GLA_HEADS, GLA_DK, GLA_DK_PAD), ((0, LANE - GLA_GATE_RANK), (0, 0))).astype(BF16)
    b2 = _pad_heads(a_gate_b, GLA_HEADS, GLA_DK, GLA_DK_PAD).reshape(1, -1)
    q, k, v, g, p, la = _inproj0(h3.reshape(bsz * t, d), norm_mix.reshape(1, d), w0, w2, b2)
    r3 = lambda z: z.reshape(bsz, t, -1)
    oa = _gla(r3(q), r3(k), r3(v), r3(g), r3(la), a_gla_norm.reshape(1, GLA_DV))
    return _outproj0(h3, oa, r3(p), a_pool_w.astype(BF16), a_pool_scale.reshape(1, -1), a_w_out.astype(BF16))


def _layer1_mixer(h3, norm_mix, c_w_in, c_cmp_pe, c_cmpk_w1, c_cmpk_w2, c_cmpv_w1, c_cmpv_w2, c_w_out):
    bsz, t, d = h3.shape
    assert t // SLC_BLOCK <= MAX_SLC and t >= WINDOW + Q_BLOCK
    nqt = t // Q_BLOCK
    cuts = (NSA_HEADS * NSA_HDIM + NSA_KV * np.arange(7)).tolist()
    wq, wkc, wvc, wks, wvs, wkw, wvw, wgt = jnp.split(c_w_in, cuts, axis=-1)
    w1 = jnp.concatenate([wq, wkc, wvc, _pad_heads(wks, NSA_KV_GROUPS, NSA_HDIM, LANE), _dup_groups(wvs),
                          _pad_heads(wkw, NSA_KV_GROUPS, NSA_HDIM, LANE), _dup_groups(wvw),
                          jnp.pad(wgt, ((0, 0), (0, LANE - wgt.shape[-1])))], axis=-1).astype(BF16)
    kfeat = jnp.asarray(_key_terms(t)).astype(BF16)
    q, kc, vc, ks, vs, kw, vw, gates = _inproj1(h3.reshape(bsz * t, d), norm_mix.reshape(1, d), w1, kfeat, t=t)
    nrow = t // CMP_STRIDE
    cfeat = jnp.asarray(_position_terms(np.arange(nrow) * CMP_STRIDE + CMP_BLOCK - 1))
    zero = jnp.zeros_like(cfeat)
    kcmp = _compress(kc.reshape(bsz, nrow, CMP_STRIDE * NSA_KV),
                     *_compress_weights(c_cmpk_w1, c_cmpk_w2, c_cmp_pe, dup=False), cfeat)
    vcmp = _compress(vc.reshape(bsz, nrow, CMP_STRIDE * NSA_KV),
                     *_compress_weights(c_cmpv_w1, c_cmpv_w2, c_cmp_pe, dup=True), zero)
    r3 = lambda z: z.reshape(bsz, t, -1)
    qfeat = jnp.asarray(_slope_terms(nqt))
    mcst = jnp.asarray(_cmp_to_slc_t(nrow)).astype(BF16)
    oc, selbias, lists = _nsa_select(qfeat, r3(q), kcmp, vcmp, mcst)
    os_, ow = _nsa_attend(lists[:, :, :, 0, :].reshape(-1), lists[:, :, :, 1, 0].reshape(-1), qfeat, r3(q), selbias,
                          r3(ks), r3(vs), r3(kw), r3(vw))
    f2 = lambda z: z.reshape(bsz * t, -1)
    out = _outproj1(h3.reshape(bsz * t, d), f2(oc), f2(os_), f2(ow), gates,
                    jnp.asarray(_gate_expand()).astype(BF16), c_w_out.astype(BF16))
    return out.reshape(bsz, t, d)


def kernel(x, norm_ffn1, ffn1_wg, ffn1_wu, ffn1_wd, norm_mix, norm_ffn2, ffn2_wg, ffn2_wu, ffn2_wd, a_w_in, a_gate_w2, a_gate_b, a_gla_norm, a_pool_w, a_pool_scale, a_w_out, c_w_in, c_cmp_pe, c_cmpk_w1, c_cmpk_w2, c_cmpv_w1, c_cmpv_w2, c_w_out, final_norm):
    bsz, t, d = x.shape
    depth = norm_ffn1.shape[0]
    fg = final_norm.reshape(1, d)
    h = x
    for l in range(depth):
        h = _ffn(h.reshape(bsz * t, d), norm_ffn1[l].reshape(1, d), ffn1_wg[l].astype(BF16), ffn1_wu[l].astype(BF16),
                 ffn1_wd[l].astype(BF16), fg, final_norm=False).reshape(bsz, t, d)
        i = l // 2
        if l % 2 == 0:
            h = _layer0_mixer(h, norm_mix[l], a_w_in[i], a_gate_w2[i], a_gate_b[i], a_gla_norm[i],
                              a_pool_w[i], a_pool_scale[i], a_w_out[i])
        else:
            h = _layer1_mixer(h, norm_mix[l], c_w_in[i], c_cmp_pe[i], c_cmpk_w1[i], c_cmpk_w2[i],
                              c_cmpv_w1[i], c_cmpv_w2[i], c_w_out[i])
        h = _ffn(h.reshape(bsz * t, d), norm_ffn2[l].reshape(1, d), ffn2_wg[l].astype(BF16), ffn2_wu[l].astype(BF16),
                 ffn2_wd[l].astype(BF16), fg, final_norm=(l == depth - 1)).reshape(bsz, t, d)
    return h
```

```python
import functools
import math

import numpy as np
import jax
import jax.numpy as jnp
from jax import lax
from jax.experimental import pallas as pl
from jax.experimental.pallas import tpu as pltpu

F32 = jnp.float32
BF16 = jnp.bfloat16

D_MODEL = 1024
D_FF = 2816
EPS = 1e-6

GLA_HEADS = 4
GLA_DV = 128
GLA_DK = 64
GLA_DK_PAD = 128
GLA_KEY = GLA_HEADS * GLA_DK
GLA_VAL = GLA_HEADS * GLA_DV
GLA_GATE_RANK = 16
GLA_TAU = 16.0
GLA_CHUNK = 64

POOL_WINDOWS = (2, 4, 8, 16)
POOL_GROUPS = 4
POOL_GDIM = 128
POOL_WIDTH = POOL_GROUPS * POOL_GDIM
POOL_HALO = 16

NSA_HDIM = 64
NSA_HEADS = 16
NSA_KV_GROUPS = 2
NSA_HPG = NSA_HEADS // NSA_KV_GROUPS
NSA_PAIRS = NSA_HPG // 2
NSA_KV = NSA_KV_GROUPS * NSA_HDIM
CMP_BLOCK = 32
CMP_STRIDE = 16
CMP_HIDDEN = 256
SLC_BLOCK = 64
SLC_SHIFT = 6
SLC_TOP = 16
WINDOW = 512
Q_BLOCK = 128
NEG = -1e30
FORCE = 1e4

LANE = 128
VMEM_LIMIT = 56 * 1024 * 1024

MAX_SLC = LANE
FEAT0 = NSA_HDIM
STAGE_BLOCKS = 8
ROWS = NSA_HPG * Q_BLOCK


def _cparams(sem):
    return pltpu.CompilerParams(dimension_semantics=sem, vmem_limit_bytes=VMEM_LIMIT)


def _rmsnorm(x, g):
    return x * lax.rsqrt(jnp.mean(x * x, axis=-1, keepdims=True) + EPS) * g


def _dot(a, b):
    return jnp.dot(a, b, preferred_element_type=F32)


def _dot_nt(a, b):
    return lax.dot_general(a, b, (((1,), (1,)), ((), ())), preferred_element_type=F32)


def _split3(x):
    hi = x.astype(BF16)
    r1 = x - hi.astype(F32)
    mid = r1.astype(BF16)
    lo = (r1 - mid.astype(F32)).astype(BF16)
    return hi, mid, lo


def _ffn_body(h_ref, g_ref, wg_ref, wu_ref, wd_ref, fg_ref, o_ref, xn_ref, acc_ref, *, final_norm):
    j = pl.program_id(1)
    last = pl.num_programs(1) - 1

    def down(xn):
        a = _dot(xn, wg_ref[...])
        u = _dot(xn, wu_ref[...])
        hid = (a * jax.nn.sigmoid(a) * u).astype(BF16)
        return _dot(hid, wd_ref[...])

    @pl.when(j == 0)
    def _():
        xn = _rmsnorm(h_ref[...], g_ref[...]).astype(BF16)
        xn_ref[...] = xn
        acc_ref[...] = down(xn)

    @pl.when((j > 0) & (j < last))
    def _():
        acc_ref[...] += down(xn_ref[...])

    @pl.when(j == last)
    def _():
        y = h_ref[...] + 0.5 * (acc_ref[...] + down(xn_ref[...]))
        if final_norm:
            y = _rmsnorm(y, fg_ref[...])
        o_ref[...] = y


def _ffn(h, gain, wg, wu, wd, final_gain, *, final_norm, tm=1024, tf=256):
    m = h.shape[0]
    assert m % tm == 0 and D_FF % tf == 0 and D_FF // tf >= 2
    return pl.pallas_call(
        functools.partial(_ffn_body, final_norm=final_norm),
        grid=(m // tm, D_FF // tf),
        in_specs=[
            pl.BlockSpec((tm, D_MODEL), lambda i, j: (i, 0)),
            pl.BlockSpec((1, D_MODEL), lambda i, j: (0, 0)),
            pl.BlockSpec((D_MODEL, tf), lambda i, j: (0, j)),
            pl.BlockSpec((D_MODEL, tf), lambda i, j: (0, j)),
            pl.BlockSpec((tf, D_MODEL), lambda i, j: (j, 0)),
            pl.BlockSpec((1, D_MODEL), lambda i, j: (0, 0)),
        ],
        out_specs=pl.BlockSpec((tm, D_MODEL), lambda i, j: (i, 0)),
        out_shape=jax.ShapeDtypeStruct((m, D_MODEL), F32),
        scratch_shapes=[pltpu.VMEM((tm, D_MODEL), BF16), pltpu.VMEM((tm, D_MODEL), F32)],
        compiler_params=_cparams(("parallel", "arbitrary")),
        name="ffn",
    )(h, gain, wg, wu, wd, final_gain)


W0_Q, W0_K, W0_V, W0_G, W0_P, W0_GR = 0, 512, 1024, 1536, 2048, 2560
W0_COLS = 2688


def _inproj0_body(h_ref, g_ref, w_ref, w2_ref, b2_ref, q_ref, k_ref, v_ref, gg_ref, p_ref, la_ref):
    xn = _rmsnorm(h_ref[...], g_ref[...]).astype(BF16)
    q_ref[...] = _dot(xn, w_ref[:, W0_Q:W0_Q + 512]).astype(BF16)
    k_ref[...] = _dot(xn, w_ref[:, W0_K:W0_K + 512]).astype(BF16)
    v_ref[...] = _dot(xn, w_ref[:, W0_V:W0_V + 512]).astype(BF16)
    gg_ref[...] = _dot(xn, w_ref[:, W0_G:W0_G + 512]).astype(BF16)
    p_ref[...] = _dot(xn, w_ref[:, W0_P:W0_P + 512]).astype(BF16)
    gr = _dot(xn, w_ref[:, W0_GR:W0_GR + LANE]).astype(BF16)
    z = _dot(gr, w2_ref[...]) + b2_ref[...]
    log_sig = jnp.minimum(z, 0.0) - jnp.log1p(jnp.exp(-jnp.abs(z)))
    la_ref[...] = log_sig / GLA_TAU


def _inproj0(h, gain, w0, w2, b2, *, tm=512):
    m = h.shape[0]
    wide = pl.BlockSpec((tm, 512), lambda i: (i, 0))
    return pl.pallas_call(
        _inproj0_body,
        grid=(m // tm,),
        in_specs=[
            pl.BlockSpec((tm, D_MODEL), lambda i: (i, 0)),
            pl.BlockSpec((1, D_MODEL), lambda i: (0, 0)),
            pl.BlockSpec((D_MODEL, W0_COLS), lambda i: (0, 0)),
            pl.BlockSpec((LANE, 512), lambda i: (0, 0)),
            pl.BlockSpec((1, 512), lambda i: (0, 0)),
        ],
        out_specs=[wide] * 6,
        out_shape=[jax.ShapeDtypeStruct((m, 512), BF16)] * 5 + [jax.ShapeDtypeStruct((m, 512), F32)],
        compiler_params=_cparams(("parallel",)),
        name="inproj0",
    )(h, gain, w0, w2, b2)


def _gla_body(q_ref, k_ref, v_ref, g_ref, la_ref, gn_ref, o_ref, st_ref, *, tt):
    c = GLA_CHUNK
    nchunk = tt // c

    @pl.when(pl.program_id(2) == 0)
    def _():
        st_ref[...] = jnp.zeros_like(st_ref)

    la = la_ref[0]
    row = lax.broadcasted_iota(jnp.int32, (tt, tt), 0)
    col = lax.broadcasted_iota(jnp.int32, (tt, tt), 1)
    shift = int(math.log2(c))
    same_chunk = (row >> shift) == (col >> shift)
    cum_m = jnp.where(same_chunk & (col <= row), 1.0, 0.0).astype(BF16)
    tot_m = jnp.where(same_chunk, 1.0, 0.0).astype(BF16)
    la3 = _split3(la)
    b = sum(_dot(cum_m, t) for t in la3)
    b_last = sum(_dot(tot_m, t) for t in la3)

    q = q_ref[0].astype(F32) * (GLA_DK ** -0.5)
    k = k_ref[0].astype(F32)
    q_dec = (q * jnp.exp(b)).astype(BF16)
    k_inv = (k * jnp.exp(-b)).astype(BF16)
    k_end = (k * jnp.exp(b_last - b)).astype(BF16)
    decay = jnp.exp(b_last)
    v_bf = v_ref[0]
    v = v_bf.astype(F32)

    ci = lax.broadcasted_iota(jnp.int32, (c, c), 0)
    cj = lax.broadcasted_iota(jnp.int32, (c, c), 1)
    causal = cj <= ci

    st = st_ref[...]
    outs = []
    for n in range(nchunk):
        sl = slice(n * c, (n + 1) * c)
        att = jnp.where(causal, _dot_nt(q_dec[sl], k_inv[sl]), 0.0)
        o = _dot(att.astype(BF16), v_bf[sl]) + _dot_nt(q_dec[sl], st.astype(BF16))
        outs.append(o)
        d_state = _dot(v[sl].T.astype(BF16), k_end[sl])
        st = decay[n * c:n * c + 1, :] * st + d_state
    st_ref[...] = st

    o = jnp.concatenate(outs, axis=0)
    o = o * lax.rsqrt(jnp.mean(o * o, axis=-1, keepdims=True) + EPS) * gn_ref[...]
    g = g_ref[0].astype(F32)
    o_ref[0] = (o * (g * jax.nn.sigmoid(g))).astype(BF16)


def _gla(q, k, v, g, la, gn, *, tt=512):
    bsz, t, _ = q.shape
    blk = pl.BlockSpec((1, tt, LANE), lambda b, h, i: (b, i, h))
    return pl.pallas_call(
        functools.partial(_gla_body, tt=tt),
        grid=(bsz, GLA_HEADS, t // tt),
        in_specs=[blk, blk, blk, blk, blk, pl.BlockSpec((1, LANE), lambda b, h, i: (0, 0))],
        out_specs=blk,
        out_shape=jax.ShapeDtypeStruct((bsz, t, GLA_VAL), BF16),
        scratch_shapes=[pltpu.VMEM((GLA_DV, GLA_DK_PAD), F32)],
        compiler_params=_cparams(("parallel", "parallel", "arbitrary")),
        name="gla",
    )(q, k, v, g, la, gn)


def _outproj0_body(h_ref, oa_ref, p_ref, pw_ref, ps_ref, wo_ref, o_ref, pbuf_ref, *, tm):
    i = pl.program_id(1)

    @pl.when(i == 0)
    def _():
        pbuf_ref[0:POOL_HALO, :] = jnp.zeros((POOL_HALO, POOL_WIDTH), F32)

    p = p_ref[0].astype(F32)
    pbuf_ref[POOL_HALO:POOL_HALO + tm, :] = p
    tpos = i * tm + lax.broadcasted_iota(jnp.int32, (tm, 1), 0)
    mix = _dot(oa_ref[0], wo_ref[0:GLA_VAL, :])
    for gi, w in enumerate(POOL_WINDOWS):
        ls = slice(gi * POOL_GDIM, (gi + 1) * POOL_GDIM)
        acc = p[:, ls]
        for s in range(1, w):
            acc = acc + pbuf_ref[POOL_HALO - s:POOL_HALO - s + tm, ls]
        cnt = jnp.minimum(tpos + 1, w).astype(F32)
        pooled = acc / cnt - p[:, ls]
        ob = _dot(pooled.astype(BF16), pw_ref[gi]) * ps_ref[:, ls]
        mix = mix + _dot(ob.astype(BF16), wo_ref[GLA_VAL + gi * POOL_GDIM:GLA_VAL + (gi + 1) * POOL_GDIM, :])
    pbuf_ref[0:POOL_HALO, :] = p[tm - POOL_HALO:, :]
    o_ref[0] = h_ref[0] + mix


def _outproj0(h, oa, p, pool_w, pool_scale, w_out, *, tm=512):
    bsz, t, _ = h.shape
    return pl.pallas_call(
        functools.partial(_outproj0_body, tm=tm),
        grid=(bsz, t // tm),
        in_specs=[
            pl.BlockSpec((1, tm, D_MODEL), lambda b, i: (b, i, 0)),
            pl.BlockSpec((1, tm, GLA_VAL), lambda b, i: (b, i, 0)),
            pl.BlockSpec((1, tm, POOL_WIDTH), lambda b, i: (b, i, 0)),
            pl.BlockSpec((POOL_GROUPS, POOL_GDIM, POOL_GDIM), lambda b, i: (0, 0, 0)),
            pl.BlockSpec((1, POOL_WIDTH), lambda b, i: (0, 0)),
            pl.BlockSpec((D_MODEL, D_MODEL), lambda b, i: (0, 0)),
        ],
        out_specs=pl.BlockSpec((1, tm, D_MODEL), lambda b, i: (b, i, 0)),
        out_shape=jax.ShapeDtypeStruct((bsz, t, D_MODEL), F32),
        scratch_shapes=[pltpu.VMEM((POOL_HALO + tm, POOL_WIDTH), F32)],
        compiler_params=_cparams(("parallel", "arbitrary")),
        name="outproj0",
    )(h, oa, p, pool_w, pool_scale, w_out)


KEY_W = 2 * LANE
VAL_W = 2 * LANE
W1_Q, W1_KC, W1_VC, W1_KS, W1_VS, W1_KW, W1_VW, W1_GT = 0, 1024, 1152, 1280, 1536, 1792, 2048, 2304
W1_COLS = 2432


def _inproj1_body(h_ref, g_ref, w_ref, kf_ref, q_ref, kc_ref, vc_ref, ks_ref, vs_ref, kw_ref, vw_ref, gt_ref):
    xn = _rmsnorm(h_ref[...], g_ref[...]).astype(BF16)
    tm = xn.shape[0]
    q_ref[...] = (_dot(xn, w_ref[:, W1_Q:W1_Q + 1024]) * (NSA_HDIM ** -0.5)).astype(BF16)
    kc_ref[...] = _dot(xn, w_ref[:, W1_KC:W1_KC + NSA_KV])
    vc_ref[...] = _dot(xn, w_ref[:, W1_VC:W1_VC + NSA_KV])
    feat = kf_ref[:, 0:LANE].astype(F32)
    onehot = kf_ref[:, LANE:2 * LANE]
    ones = jnp.ones((tm, LANE), BF16)
    for g in range(NSA_KV_GROUPS):
        gl = slice(g * LANE, (g + 1) * LANE)
        ks = (_dot(xn, w_ref[:, W1_KS + g * LANE:W1_KS + (g + 1) * LANE]) + feat).astype(BF16)
        ks_ref[:, g * KEY_W:g * KEY_W + LANE] = ks
        ks_ref[:, g * KEY_W + LANE:(g + 1) * KEY_W] = onehot
        kw_ref[:, gl] = (_dot(xn, w_ref[:, W1_KW + g * LANE:W1_KW + (g + 1) * LANE]) + feat).astype(BF16)
        vs_ref[:, g * VAL_W:g * VAL_W + LANE] = _dot(xn, w_ref[:, W1_VS + g * LANE:W1_VS + (g + 1) * LANE]).astype(BF16)
        vs_ref[:, g * VAL_W + LANE:(g + 1) * VAL_W] = ones
        vw_ref[:, g * VAL_W:g * VAL_W + LANE] = _dot(xn, w_ref[:, W1_VW + g * LANE:W1_VW + (g + 1) * LANE]).astype(BF16)
        vw_ref[:, g * VAL_W + LANE:(g + 1) * VAL_W] = ones
    gt_ref[...] = jax.nn.sigmoid(_dot(xn, w_ref[:, W1_GT:W1_GT + LANE]))


def _inproj1(h, gain, w1, kfeat, *, t, tm=512):
    m = h.shape[0]
    per_seq = t // tm
    spec = lambda n: pl.BlockSpec((tm, n), lambda i: (i, 0))
    shp = lambda n, dt: jax.ShapeDtypeStruct((m, n), dt)
    g = NSA_KV_GROUPS
    return pl.pallas_call(
        _inproj1_body,
        grid=(m // tm,),
        in_specs=[
            pl.BlockSpec((tm, D_MODEL), lambda i: (i, 0)),
            pl.BlockSpec((1, D_MODEL), lambda i: (0, 0)),
            pl.BlockSpec((D_MODEL, W1_COLS), lambda i: (0, 0)),
            pl.BlockSpec((tm, KEY_W), lambda i: (i % per_seq, 0)),
        ],
        out_specs=[spec(1024), spec(NSA_KV), spec(NSA_KV), spec(g * KEY_W), spec(g * VAL_W), spec(g * LANE),
                   spec(g * VAL_W), spec(LANE)],
        out_shape=[shp(1024, BF16), shp(NSA_KV, F32), shp(NSA_KV, F32), shp(g * KEY_W, BF16), shp(g * VAL_W, BF16),
                   shp(g * LANE, BF16), shp(g * VAL_W, BF16), shp(LANE, F32)],
        compiler_params=_cparams(("parallel",)),
        name="inproj1",
    )(h, gain, w1, kfeat)


def _compress_body(z_ref, pea_ref, peb_ref, wa_ref, wb_ref, w2_ref, cf_ref, o_ref, sh_ref, *, nrow):
    z = z_ref[0]
    a = _dot((z + pea_ref[...]).astype(BF16), wa_ref[...])
    bm = _dot((z + peb_ref[...]).astype(BF16), wb_ref[...])
    sh_ref[0:nrow, :] = bm
    sh_ref[nrow:nrow + 8, :] = jnp.zeros((8, sh_ref.shape[1]), F32)
    x = a + sh_ref[1:nrow + 1, :]
    cdf = 0.5 * (1.0 + jnp.tanh(math.sqrt(2.0 / math.pi) * (x + 0.044715 * (x * x * x))))
    hid = (x * cdf).astype(BF16)
    for g in range(NSA_KV_GROUPS):
        o_ref[0, g] = (_dot(hid[:, g * CMP_HIDDEN:(g + 1) * CMP_HIDDEN], w2_ref[...]) + cf_ref[...]).astype(BF16)


def _compress(z, pea, peb, wa, wb, w2, cfeat):
    bsz, nrow, zw = z.shape
    hw = NSA_KV_GROUPS * CMP_HIDDEN
    return pl.pallas_call(
        functools.partial(_compress_body, nrow=nrow),
        grid=(bsz,),
        in_specs=[
            pl.BlockSpec((1, nrow, zw), lambda b: (b, 0, 0)),
            pl.BlockSpec((1, zw), lambda b: (0, 0)),
            pl.BlockSpec((1, zw), lambda b: (0, 0)),
            pl.BlockSpec((zw, hw), lambda b: (0, 0)),
            pl.BlockSpec((zw, hw), lambda b: (0, 0)),
            pl.BlockSpec((CMP_HIDDEN, LANE), lambda b: (0, 0)),
            pl.BlockSpec((nrow, LANE), lambda b: (0, 0)),
        ],
        out_specs=pl.BlockSpec((1, NSA_KV_GROUPS, nrow, LANE), lambda b: (b, 0, 0, 0)),
        out_shape=jax.ShapeDtypeStruct((bsz, NSA_KV_GROUPS, nrow, LANE), BF16),
        scratch_shapes=[pltpu.VMEM((nrow + 8, hw), F32)],
        compiler_params=_cparams(("parallel",)),
        name="compress",
    )(z, pea, peb, wa, wb, w2, cfeat)


def _fill_query_rows(qa_ref, q_ref, qf_ref, u=0):
    lane_lo = lax.broadcasted_iota(jnp.int32, (Q_BLOCK, LANE), 1) < NSA_HDIM
    for pr in range(NSA_PAIRS):
        even = q_ref[0, u * Q_BLOCK:(u + 1) * Q_BLOCK, pr * LANE:(pr + 1) * LANE]
        odd = pltpu.roll(even.astype(F32), NSA_HDIM, axis=1).astype(BF16)
        for hh, src in enumerate((even, odd)):
            h = 2 * pr + hh
            feat = jnp.broadcast_to(qf_ref[0, u, h:h + 1, :], (Q_BLOCK, LANE)).astype(BF16)
            qa_ref[h * Q_BLOCK:(h + 1) * Q_BLOCK, 0:LANE] = jnp.where(lane_lo, src, feat)


def _merge_pairs(o_ref, o, u=0):
    lane_lo = lax.broadcasted_iota(jnp.int32, (Q_BLOCK, LANE), 1) < NSA_HDIM
    for pr in range(NSA_PAIRS):
        a = o[(2 * pr) * Q_BLOCK:(2 * pr + 1) * Q_BLOCK]
        b = o[(2 * pr + 1) * Q_BLOCK:(2 * pr + 2) * Q_BLOCK]
        o_ref[0, u * Q_BLOCK:(u + 1) * Q_BLOCK, pr * LANE:(pr + 1) * LANE] = jnp.where(lane_lo, a, b).astype(o_ref.dtype)


def _head_tile(bias):
    return jnp.concatenate([bias] * NSA_HPG, axis=0)


SEL_TILES = 4


def _compressed_branch(u, qf_ref, q_ref, kc_ref, vc_ref, mcst_ref, oc_ref, qa_ref):
    t0 = (pl.program_id(2) * SEL_TILES + u) * Q_BLOCK
    ncmp_pad = kc_ref.shape[2]
    nq = Q_BLOCK
    _fill_query_rows(qa_ref, q_ref, qf_ref, u)
    tpos = t0 + lax.broadcasted_iota(jnp.int32, (nq, 1), 0)
    cmp_end = lax.broadcasted_iota(jnp.int32, (1, ncmp_pad), 1) * CMP_STRIDE + (CMP_BLOCK - 1)
    bias_c = jnp.where(tpos >= cmp_end, 0.0, NEG)
    any_c = jnp.where(tpos >= (CMP_BLOCK - 1), 1.0, 0.0)
    s = _dot_nt(qa_ref[...], kc_ref[0, 0]) + _head_tile(bias_c)
    e = jnp.exp(s - jnp.max(s, axis=-1, keepdims=True))
    p = e * (_head_tile(any_c) / jnp.sum(e, axis=-1, keepdims=True))
    _merge_pairs(oc_ref, _dot(p.astype(BF16), vc_ref[0, 0]), u)
    p_sum = p[0:nq]
    for h in range(1, NSA_HPG):
        p_sum = p_sum + p[h * nq:(h + 1) * nq]

    return sum(_dot_nt(mcst_ref[...], term) for term in _split3(p_sum))


def _nsa_select_body(qf_ref, q_ref, kc_ref, vc_ref, mcst_ref, oc_ref, sb_ref, lst_ref, *qa_refs):
    nt = SEL_TILES
    nq = Q_BLOCK
    wide = nt * nq
    imp = jnp.concatenate([_compressed_branch(u, qf_ref, q_ref, kc_ref, vc_ref, mcst_ref, oc_ref, qa_refs[u])
                           for u in range(nt)], axis=1)

    blk = lax.broadcasted_iota(jnp.int32, (MAX_SLC, wide), 0)
    tpos_l = pl.program_id(2) * wide + lax.broadcasted_iota(jnp.int32, (MAX_SLC, wide), 1)
    cur = tpos_l >> SLC_SHIFT
    forced = (blk == 0) | (blk == cur) | (blk == cur - 1)
    valid_b = blk * SLC_BLOCK <= tpos_l
    score = jnp.where(valid_b, jnp.where(forced, FORCE, imp), -1.0)
    blk_f = blk.astype(F32)
    sel = jnp.zeros((MAX_SLC, wide), F32)
    for _ in range(SLC_TOP):
        mx = jnp.max(score, axis=0, keepdims=True)
        first = jnp.min(jnp.where(score == mx, blk_f, float(MAX_SLC)), axis=0, keepdims=True)
        pick = blk_f == first
        sel = jnp.where(pick, 1.0, sel)
        score = jnp.where(pick, -2.0, score)
    own = (tpos_l >> (SLC_SHIFT + 1)) << 1
    sel = jnp.where(valid_b & (blk < own), sel, 0.0)
    for u in range(nt):
        sel_u = sel[:, u * nq:(u + 1) * nq]
        sb_ref[0, 0, u * nq:(u + 1) * nq, :] = jnp.where(sel_u.T > 0.5, 0.0, NEG).astype(BF16)

    used_b = jnp.concatenate(
        [jnp.broadcast_to(jnp.max(sel[:, u * nq:(u + 1) * nq], axis=1, keepdims=True), (MAX_SLC, LANE))
         for u in range(nt)], axis=1).astype(BF16)
    r = lax.broadcasted_iota(jnp.int32, (MAX_SLC, LANE), 0)
    c = lax.broadcasted_iota(jnp.int32, (MAX_SLC, LANE), 1)
    before = jnp.where(c < r, 1.0, 0.0).astype(BF16)
    pos = _dot(before, used_b)
    slot_w = (lax.broadcasted_iota(jnp.int32, (MAX_SLC, nt * LANE), 1) & (LANE - 1)).astype(F32)
    place = jnp.where((pos == slot_w) & (used_b > 0.5), 1.0, 0.0).astype(BF16)
    ids = lax.broadcasted_iota(jnp.int32, (8, LANE), 1).astype(F32).astype(BF16)
    lst = _dot(ids, place)
    cnt = _dot(jnp.ones((8, LANE), BF16), used_b)
    lane8 = lax.broadcasted_iota(jnp.int32, (8, nt * LANE), 1)
    filler = ((pl.program_id(2) * nt + (lane8 >> 7)) << 1).astype(F32)
    lst = jnp.where((lane8 & (LANE - 1)).astype(F32) < cnt, lst, filler)
    row0 = lax.broadcasted_iota(jnp.int32, (8, nt * LANE), 0) == 0
    out = jnp.where(row0, lst, cnt).astype(jnp.int32)
    for u in range(nt):
        lst_ref[0, 0, u] = out[:, u * LANE:(u + 1) * LANE]


def _nsa_select(qfeat, q, kcmp, vcmp, mcst):
    bsz, t, _ = q.shape
    ncmp_pad = kcmp.shape[2]
    nqt = t // Q_BLOCK
    gw = NSA_HPG * NSA_HDIM
    assert nqt % SEL_TILES == 0
    nt = SEL_TILES
    qspec = pl.BlockSpec((1, nt * Q_BLOCK, gw), lambda b, g, i: (b, i, g))
    cspec = pl.BlockSpec((1, 1, ncmp_pad, LANE), lambda b, g, i: (b, g, 0, 0))
    return pl.pallas_call(
        _nsa_select_body,
        grid=(bsz, NSA_KV_GROUPS, nqt // nt),
        in_specs=[
            pl.BlockSpec((1, nt, NSA_HPG, LANE), lambda b, g, i: (g, i, 0, 0)),
            qspec, cspec, cspec,
            pl.BlockSpec((MAX_SLC, ncmp_pad), lambda b, g, i: (0, 0)),
        ],
        out_specs=[qspec,
                   pl.BlockSpec((1, 1, nt * Q_BLOCK, MAX_SLC), lambda b, g, i: (b, g, i, 0)),
                   pl.BlockSpec((1, 1, nt, 8, LANE), lambda b, g, i: (b, g, i, 0, 0))],
        out_shape=[jax.ShapeDtypeStruct((bsz, t, NSA_HEADS * NSA_HDIM), BF16),
                   jax.ShapeDtypeStruct((bsz, NSA_KV_GROUPS, t, MAX_SLC), BF16),
                   jax.ShapeDtypeStruct((bsz, NSA_KV_GROUPS, nqt, 8, LANE), jnp.int32)],
        scratch_shapes=[pltpu.VMEM((ROWS, LANE), BF16)] * nt,
        compiler_params=_cparams(("parallel", "parallel", "parallel")),
        name="nsa_select",
    )(qfeat, q, kcmp, vcmp, mcst)


def _nsa_attend_body(lst_ref, cnt_ref, qf_ref, q_ref, sb_ref, ks_ref, vs_ref, kw_ref, vw_ref, os_ref, ow_ref,
                     qa_ref, kst0_ref, kst1_ref, vst0_ref, vst1_ref, s0_ref, s1_ref, m_ref, acc_ref):
    b = pl.program_id(0)
    g = pl.program_id(1)
    qi = pl.program_id(2)
    nqt = pl.num_programs(2)
    t0 = pl.multiple_of(qi * Q_BLOCK, Q_BLOCK)
    nq = Q_BLOCK
    _fill_query_rows(qa_ref, q_ref, qf_ref)
    sb = sb_ref[0, 0]
    for h in range(NSA_HPG):
        qa_ref[h * nq:(h + 1) * nq, LANE:2 * LANE] = sb
    qa_lo = qa_ref[:, 0:LANE]

    tile = (b * NSA_KV_GROUPS + g) * nqt + qi
    base = tile * LANE
    count = cnt_ref[tile]

    bufs = ((kst0_ref, vst0_ref, s0_ref), (kst1_ref, vst1_ref, s1_ref))

    def stage_and_score(c, slot):
        kst_ref, vst_ref, s_ref = bufs[slot]
        for j in range(STAGE_BLOCKS):
            idx = jnp.minimum(c * STAGE_BLOCKS + j, LANE - 1)
            r0 = pl.multiple_of(lst_ref[base + idx] * SLC_BLOCK, SLC_BLOCK)
            kst_ref[j * SLC_BLOCK:(j + 1) * SLC_BLOCK, :] = ks_ref[0, pl.ds(r0, SLC_BLOCK), :]
            vst_ref[j * SLC_BLOCK:(j + 1) * SLC_BLOCK, :] = vs_ref[0, pl.ds(r0, SLC_BLOCK), :]
        s_ref[...] = _dot_nt(qa_ref[...], kst_ref[...])

    def softmax_pv(slot):
        _, vst_ref, s_ref = bufs[slot]
        s = s_ref[...]
        m_old = m_ref[...]
        m_new = jnp.maximum(m_old, jnp.max(s, axis=-1, keepdims=True))
        acc_ref[...] = jnp.exp(m_old - m_new) * acc_ref[...] + _dot(jnp.exp(s - m_new).astype(BF16), vst_ref[...])
        m_ref[...] = m_new

    wk = WINDOW + nq
    w0 = pl.multiple_of(jnp.maximum(t0 - WINDOW, 0), nq)
    dist = (t0 + lax.broadcasted_iota(jnp.int32, (nq, 1), 0)) - (w0 + lax.broadcasted_iota(jnp.int32, (1, wk), 1))
    bias_w = jnp.where((dist >= 0) & (dist < WINDOW), 0.0, NEG)
    s = _dot_nt(qa_lo, kw_ref[0, pl.ds(w0, wk), :]) + _head_tile(bias_w)
    e = jnp.exp(s - jnp.max(s, axis=-1, keepdims=True))
    acc = _dot(e.astype(BF16), vw_ref[0, pl.ds(w0, wk), :])
    _merge_pairs(ow_ref, acc[:, 0:LANE] / acc[:, LANE:2 * LANE])

    ti = lax.broadcasted_iota(jnp.int32, (nq, nq), 0)
    tj = lax.broadcasted_iota(jnp.int32, (nq, nq), 1)
    s = _dot_nt(qa_lo, ks_ref[0, pl.ds(t0, nq), 0:LANE]) + _head_tile(jnp.where(tj <= ti, 0.0, NEG))
    m = jnp.max(s, axis=-1, keepdims=True)
    m_ref[...] = m
    acc_ref[...] = _dot(jnp.exp(s - m).astype(BF16), vs_ref[0, pl.ds(t0, nq), :])

    stage_and_score(0, 0)

    n_groups = (count + STAGE_BLOCKS - 1) // STAGE_BLOCKS

    def step(c, carry):
        stage_and_score(2 * c + 1, 1)
        softmax_pv(0)

        @pl.when(2 * c + 1 < n_groups)
        def _():
            stage_and_score(2 * c + 2, 0)
            softmax_pv(1)

        return carry

    lax.fori_loop(0, (n_groups + 1) // 2, step, 0)
    acc = acc_ref[...]
    _merge_pairs(os_ref, acc[:, 0:LANE] / acc[:, LANE:2 * LANE])


def _nsa_attend(lists, counts, qfeat, q, selbias, ks, vs, kw, vw):
    bsz, t, _ = q.shape
    nqt = t // Q_BLOCK
    gw = NSA_HPG * NSA_HDIM
    qspec = pl.BlockSpec((1, Q_BLOCK, gw), lambda b, g, i, lst, cnt: (b, i, g))
    out = jax.ShapeDtypeStruct((bsz, t, NSA_HEADS * NSA_HDIM), BF16)
    stage = STAGE_BLOCKS * SLC_BLOCK
    return pl.pallas_call(
        _nsa_attend_body,
        grid_spec=pltpu.PrefetchScalarGridSpec(
            num_scalar_prefetch=2,
            grid=(bsz, NSA_KV_GROUPS, nqt),
            in_specs=[
                pl.BlockSpec((1, 1, NSA_HPG, LANE), lambda b, g, i, lst, cnt: (g, i, 0, 0)),
                qspec,
                pl.BlockSpec((1, 1, Q_BLOCK, MAX_SLC), lambda b, g, i, lst, cnt: (b, g, i, 0)),
                pl.BlockSpec((1, t, KEY_W), lambda b, g, i, lst, cnt: (b, 0, g)),
                pl.BlockSpec((1, t, VAL_W), lambda b, g, i, lst, cnt: (b, 0, g)),
                pl.BlockSpec((1, t, LANE), lambda b, g, i, lst, cnt: (b, 0, g)),
                pl.BlockSpec((1, t, VAL_W), lambda b, g, i, lst, cnt: (b, 0, g)),
            ],
            out_specs=[qspec, qspec],
            scratch_shapes=[
                pltpu.VMEM((ROWS, KEY_W), BF16),
                pltpu.VMEM((stage, KEY_W), BF16),
                pltpu.VMEM((stage, KEY_W), BF16),
                pltpu.VMEM((stage, VAL_W), BF16),
                pltpu.VMEM((stage, VAL_W), BF16),
                pltpu.VMEM((ROWS, stage), F32),
                pltpu.VMEM((ROWS, stage), F32),
                pltpu.VMEM((ROWS, 1), F32),
                pltpu.VMEM((ROWS, VAL_W), F32),
            ],
        ),
        out_shape=[out, out],
        compiler_params=_cparams(("parallel", "parallel", "arbitrary")),
        name="nsa_attend",
    )(lists, counts, qfeat, q, selbias, ks, vs, kw, vw)


def _outproj1_body(h_ref, oc_ref, os_ref, ow_ref, gt_ref, ex_ref, wo_ref, o_ref):
    gt = gt_ref[...]
    hi = gt.astype(BF16)
    lo = (gt - hi.astype(F32)).astype(BF16)
    o = jnp.zeros(oc_ref.shape, F32)
    for c, br in enumerate((oc_ref, os_ref, ow_ref)):
        ge = _dot(hi, ex_ref[c]) + _dot(lo, ex_ref[c])
        o = o + ge * br[...].astype(F32)
    o_ref[...] = h_ref[...] + _dot(o.astype(BF16), wo_ref[...])


def _outproj1(h, oc, os_, ow, gates, expand, w_out, *, tm=512):
    m = h.shape[0]
    row = pl.BlockSpec((tm, D_MODEL), lambda i: (i, 0))
    return pl.pallas_call(
        _outproj1_body,
        grid=(m // tm,),
        in_specs=[row, row, row, row,
                  pl.BlockSpec((tm, LANE), lambda i: (i, 0)),
                  pl.BlockSpec((3, LANE, D_MODEL), lambda i: (0, 0, 0)),
                  pl.BlockSpec((D_MODEL, D_MODEL), lambda i: (0, 0))],
        out_specs=row,
        out_shape=jax.ShapeDtypeStruct((m, D_MODEL), F32),
        compiler_params=_cparams(("parallel",)),
        name="outproj1",
    )(h, oc, os_, ow, gates, expand, w_out)


def _pad_heads(w, heads, width, padded):
    lead = w.shape[:-1]
    w = w.reshape(lead + (heads, width))
    w = jnp.pad(w, [(0, 0)] * len(lead) + [(0, 0), (0, padded - width)])
    return w.reshape(lead + (heads * padded,))


def _dup_groups(w):
    lead = w.shape[:-1]
    w = w.reshape(lead + (NSA_KV_GROUPS, NSA_HDIM))
    return jnp.concatenate([w, w], axis=-1).reshape(lead + (NSA_KV_GROUPS * LANE,))


def _np_split3(x):
    out = []
    r = np.asarray(x, np.float64)
    for _ in range(3):
        part = np.asarray(np.asarray(r, np.float32).astype(jnp.bfloat16), np.float64)
        out.append(part.astype(np.float32))
        r = r - part
    return out


def _position_terms(pos):
    pos = np.asarray(pos)
    f = np.zeros((pos.shape[0], LANE), np.float32)
    f[:, FEAT0 + 0:FEAT0 + 3] = (pos >> SLC_SHIFT)[:, None]
    f[:, FEAT0 + 3:FEAT0 + 6] = (pos & (SLC_BLOCK - 1))[:, None]
    f[:, FEAT0 + 6:FEAT0 + 9] = 1.0
    return f


def _slope_terms(nqt):
    slopes = np.array([2.0 ** (-8.0 * (i + 1) / NSA_HEADS) for i in range(NSA_HEADS)], dtype=np.float32)
    slopes = slopes.reshape(NSA_HPG, NSA_KV_GROUPS).T.astype(np.float64)
    f = np.zeros((NSA_KV_GROUPS, nqt, NSA_HPG, LANE), np.float32)
    t0 = (np.arange(nqt) * Q_BLOCK).astype(np.float64)
    for k, part in enumerate(_np_split3(slopes * SLC_BLOCK)):
        f[:, :, :, FEAT0 + k] = part[:, None, :]
    for k, part in enumerate(_np_split3(slopes)):
        f[:, :, :, FEAT0 + 3 + k] = part[:, None, :]
    for k, part in enumerate(_np_split3(-slopes[:, None, :] * t0[None, :, None])):
        f[:, :, :, FEAT0 + 6 + k] = part
    return f


def _key_terms(t):
    pos = np.arange(t)
    onehot = np.zeros((t, MAX_SLC), np.float32)
    onehot[pos, pos >> SLC_SHIFT] = 1.0
    return np.concatenate([_position_terms(pos), onehot], axis=1)


def _cmp_to_slc_t(ncmp_pad):
    cs = np.arange(ncmp_pad) * CMP_STRIDE
    ss = np.arange(MAX_SLC) * SLC_BLOCK
    ov = np.minimum(cs[None] + CMP_BLOCK, ss[:, None] + SLC_BLOCK) - np.maximum(cs[None], ss[:, None])
    return (np.clip(ov, 0, None) / CMP_BLOCK).astype(np.float32)


def _gate_expand():
    e = np.zeros((3, LANE, NSA_HEADS * NSA_HDIM), np.float32)
    for c in range(3):
        for g in range(NSA_KV_GROUPS):
            for h in range(NSA_HPG):
                col = g * NSA_HPG * 3 + h * 3 + c
                base = (g * NSA_HPG + h) * NSA_HDIM
                e[c, col, base:base + NSA_HDIM] = 1.0
    return e


def _compress_weights(w1, w2, pe, dup):
    half = CMP_BLOCK // 2
    eye = jnp.eye(NSA_KV_GROUPS, dtype=w1.dtype)
    w1 = w1.reshape(2, half, NSA_HDIM, CMP_HIDDEN)
    wexp = jnp.einsum('sjdc,gh->sjgdhc', w1, eye).reshape(2, half * NSA_KV, NSA_KV_GROUPS * CMP_HIDDEN)
    pe = jnp.broadcast_to(pe.reshape(2, half, 1, NSA_HDIM), (2, half, NSA_KV_GROUPS, NSA_HDIM)).reshape(2, 1, half * NSA_KV)
    w2w = jnp.concatenate([w2, w2 if dup else jnp.zeros_like(w2)], axis=-1)
    return pe[0], pe[1], wexp[0].astype(BF16), wexp[1].astype(BF16), w2w.astype(BF16)


def _layer0_mixer(h3, norm_mix, a_w_in, a_gate_w2, a_gate_b, a_gla_norm, a_pool_w, a_pool_scale, a_w_out):
    bsz, t, d = h3.shape
    cuts = np.cumsum([GLA_KEY, GLA_KEY, GLA_VAL, GLA_VAL, GLA_GATE_RANK]).tolist()
    wq, wk, wv, wg, wgr, wp = jnp.split(a_w_in, cuts, axis=-1)
    w0 = jnp.concatenate([
        _pad_heads(wq, GLA_HEADS, GLA_DK, GLA_DK_PAD), _pad_heads(wk, GLA_HEADS, GLA_DK, GLA_DK_PAD),
        wv, wg, wp, jnp.pad(wgr, ((0, 0), (0, LANE - GLA_GATE_RANK)))], axis=-1).astype(BF16)
    w2 = jnp.pad(_pad_heads(a_gate_w2, GLA_HEADS, GLA_DK, GLA_DK_PAD), ((0, LANE - GLA_GATE_RANK), (0, 0))).astype(BF16)
    b2 = _pad_heads(a_gate_b, GLA_HEADS, GLA_DK, GLA_DK_PAD).reshape(1, -1)
    q, k, v, g, p, la = _inproj0(h3.reshape(bsz * t, d), norm_mix.reshape(1, d), w0, w2, b2)
    r3 = lambda z: z.reshape(bsz, t, -1)
    oa = _gla(r3(q), r3(k), r3(v), r3(g), r3(la), a_gla_norm.reshape(1, GLA_DV))
    return _outproj0(h3, oa, r3(p), a_pool_w.astype(BF16), a_pool_scale.reshape(1, -1), a_w_out.astype(BF16))


def _layer1_mixer(h3, norm_mix, c_w_in, c_cmp_pe, c_cmpk_w1, c_cmpk_w2, c_cmpv_w1, c_cmpv_w2, c_w_out):
    bsz, t, d = h3.shape
    assert t // SLC_BLOCK <= MAX_SLC and t >= WINDOW + Q_BLOCK
    nqt = t // Q_BLOCK
    cuts = (NSA_HEADS * NSA_HDIM + NSA_KV * np.arange(7)).tolist()
    wq, wkc, wvc, wks, wvs, wkw, wvw, wgt = jnp.split(c_w_in, cuts, axis=-1)
    w1 = jnp.concatenate([wq, wkc, wvc, _pad_heads(wks, NSA_KV_GROUPS, NSA_HDIM, LANE), _dup_groups(wvs),
                          _pad_heads(wkw, NSA_KV_GROUPS, NSA_HDIM, LANE), _dup_groups(wvw),
                          jnp.pad(wgt, ((0, 0), (0, LANE - wgt.shape[-1])))], axis=-1).astype(BF16)
    kfeat = jnp.asarray(_key_terms(t)).astype(BF16)
    q, kc, vc, ks, vs, kw, vw, gates = _inproj1(h3.reshape(bsz * t, d), norm_mix.reshape(1, d), w1, kfeat, t=t)
    nrow = t // CMP_STRIDE
    cfeat = jnp.asarray(_position_terms(np.arange(nrow) * CMP_STRIDE + CMP_BLOCK - 1))
    zero = jnp.zeros_like(cfeat)
    kcmp = _compress(kc.reshape(bsz, nrow, CMP_STRIDE * NSA_KV),
                     *_compress_weights(c_cmpk_w1, c_cmpk_w2, c_cmp_pe, dup=False), cfeat)
    vcmp = _compress(vc.reshape(bsz, nrow, CMP_STRIDE * NSA_KV),
                     *_compress_weights(c_cmpv_w1, c_cmpv_w2, c_cmp_pe, dup=True), zero)
    r3 = lambda z: z.reshape(bsz, t, -1)
    qfeat = jnp.asarray(_slope_terms(nqt))
    mcst = jnp.asarray(_cmp_to_slc_t(nrow)).astype(BF16)
    oc, selbias, lists = _nsa_select(qfeat, r3(q), kcmp, vcmp, mcst)
    os_, ow = _nsa_attend(lists[:, :, :, 0, :].reshape(-1), lists[:, :, :, 1, 0].reshape(-1), qfeat, r3(q), selbias,
                          r3(ks), r3(vs), r3(kw), r3(vw))
    f2 = lambda z: z.reshape(bsz * t, -1)
    out = _outproj1(h3.reshape(bsz * t, d), f2(oc), f2(os_), f2(ow), gates,
                    jnp.asarray(_gate_expand()).astype(BF16), c_w_out.astype(BF16))
    return out.reshape(bsz, t, d)


def kernel(x, norm_ffn1, ffn1_wg, ffn1_wu, ffn1_wd, norm_mix, norm_ffn2, ffn2_wg, ffn2_wu, ffn2_wd, a_w_in, a_gate_w2, a_gate_b, a_gla_norm, a_pool_w, a_pool_scale, a_w_out, c_w_in, c_cmp_pe, c_cmpk_w1, c_cmpk_w2, c_cmpv_w1, c_cmpv_w2, c_w_out, final_norm):
    bsz, t, d = x.shape
    depth = norm_ffn1.shape[0]
    fg = final_norm.reshape(1, d)
    h = x
    for l in range(depth):
        h = _ffn(h.reshape(bsz * t, d), norm_ffn1[l].reshape(1, d), ffn1_wg[l].astype(BF16), ffn1_wu[l].astype(BF16),
                 ffn1_wd[l].astype(BF16), fg, final_norm=False).reshape(bsz, t, d)
        i = l // 2
        if l % 2 == 0:
            h = _layer0_mixer(h, norm_mix[l], a_w_in[i], a_gate_w2[i], a_gate_b[i], a_gla_norm[i],
                              a_pool_w[i], a_pool_scale[i], a_w_out[i])
        else:
            h = _layer1_mixer(h, norm_mix[l], c_w_in[i], c_cmp_pe[i], c_cmpk_w1[i], c_cmpk_w2[i],
                              c_cmpv_w1[i], c_cmpv_w2[i], c_w_out[i])
        h = _ffn(h.reshape(bsz * t, d), norm_ffn2[l].reshape(1, d), ffn2_wg[l].astype(BF16), ffn2_wu[l].astype(BF16),
                 ffn2_wd[l].astype(BF16), fg, final_norm=(l == depth - 1)).reshape(bsz, t, d)
    return h
```

```python
import functools
import math

import numpy as np
import jax
import jax.numpy as jnp
from jax import lax
from jax.experimental import pallas as pl
from jax.experimental.pallas import tpu as pltpu

F32 = jnp.float32
BF16 = jnp.bfloat16

D_MODEL = 1024
D_FF = 2816
EPS = 1e-6

GLA_HEADS = 4
GLA_DV = 128
GLA_DK = 64
GLA_DK_PAD = 128
GLA_KEY = GLA_HEADS * GLA_DK
GLA_VAL = GLA_HEADS * GLA_DV
GLA_GATE_RANK = 16
GLA_TAU = 16.0
GLA_CHUNK = 64

POOL_WINDOWS = (2, 4, 8, 16)
POOL_GROUPS = 4
POOL_GDIM = 128
POOL_WIDTH = POOL_GROUPS * POOL_GDIM
POOL_HALO = 16

NSA_HDIM = 64
NSA_HEADS = 16
NSA_KV_GROUPS = 2
NSA_HPG = NSA_HEADS // NSA_KV_GROUPS
NSA_PAIRS = NSA_HPG // 2
NSA_KV = NSA_KV_GROUPS * NSA_HDIM
CMP_BLOCK = 32
CMP_STRIDE = 16
CMP_HIDDEN = 256
SLC_BLOCK = 64
SLC_SHIFT = 6
SLC_TOP = 16
WINDOW = 512
Q_BLOCK = 128
NEG = -1e30
FORCE = 1e4

LANE = 128
VMEM_LIMIT = 56 * 1024 * 1024

MAX_SLC = LANE
FEAT0 = NSA_HDIM
STAGE_BLOCKS = 8
ROWS = NSA_HPG * Q_BLOCK


def _cparams(sem):
    return pltpu.CompilerParams(dimension_semantics=sem, vmem_limit_bytes=VMEM_LIMIT)


def _rmsnorm(x, g):
    return x * lax.rsqrt(jnp.mean(x * x, axis=-1, keepdims=True) + EPS) * g


def _dot(a, b):
    return jnp.dot(a, b, preferred_element_type=F32)


def _dot_nt(a, b):
    return lax.dot_general(a, b, (((1,), (1,)), ((), ())), preferred_element_type=F32)


def _split3(x):
    hi = x.astype(BF16)
    r1 = x - hi.astype(F32)
    mid = r1.astype(BF16)
    lo = (r1 - mid.astype(F32)).astype(BF16)
    return hi, mid, lo


def _ffn_body(h_ref, g_ref, wg_ref, wu_ref, wd_ref, fg_ref, o_ref, xn_ref, acc_ref, *, final_norm):
    j = pl.program_id(1)
    last = pl.num_programs(1) - 1

    def down(xn):
        a = _dot(xn, wg_ref[0])
        u = _dot(xn, wu_ref[0])
        hid = (a * jax.nn.sigmoid(a) * u).astype(BF16)
        return _dot(hid, wd_ref[...])

    @pl.when(j == 0)
    def _():
        xn = _rmsnorm(h_ref[...], g_ref[...]).astype(BF16)
        xn_ref[...] = xn
        acc_ref[...] = down(xn)

    @pl.when((j > 0) & (j < last))
    def _():
        acc_ref[...] += down(xn_ref[...])

    @pl.when(j == last)
    def _():
        y = h_ref[...] + 0.5 * (acc_ref[...] + down(xn_ref[...]))
        if final_norm:
            y = _rmsnorm(y, fg_ref[...])
        o_ref[...] = y


def _ffn(h, gain, wg, wu, wd, final_gain, *, final_norm, tm=1024, tf=256):
    m = h.shape[0]
    assert m % tm == 0 and D_FF % tf == 0 and D_FF // tf >= 2
    return pl.pallas_call(
        functools.partial(_ffn_body, final_norm=final_norm),
        grid=(m // tm, D_FF // tf),
        in_specs=[
            pl.BlockSpec((tm, D_MODEL), lambda i, j: (i, 0)),
            pl.BlockSpec((1, D_MODEL), lambda i, j: (0, 0)),
            pl.BlockSpec((1, D_MODEL, tf), lambda i, j: (j, 0, 0)),
            pl.BlockSpec((1, D_MODEL, tf), lambda i, j: (j, 0, 0)),
            pl.BlockSpec((tf, D_MODEL), lambda i, j: (j, 0)),
            pl.BlockSpec((1, D_MODEL), lambda i, j: (0, 0)),
        ],
        out_specs=pl.BlockSpec((tm, D_MODEL), lambda i, j: (i, 0)),
        out_shape=jax.ShapeDtypeStruct((m, D_MODEL), F32),
        scratch_shapes=[pltpu.VMEM((tm, D_MODEL), BF16), pltpu.VMEM((tm, D_MODEL), F32)],
        compiler_params=_cparams(("parallel", "arbitrary")),
        name="ffn",
    )(h, gain, _chunk_cols(wg, tf), _chunk_cols(wu, tf), wd.astype(BF16), final_gain)


def _chunk_cols(w, tf):
    d, f = w.shape
    return w.astype(BF16).reshape(d, f // tf, tf).transpose(1, 0, 2)


W0_Q, W0_K, W0_V, W0_G, W0_P, W0_GR = 0, 512, 1024, 1536, 2048, 2560
W0_COLS = 2688


def _inproj0_body(h_ref, g_ref, w_ref, w2_ref, b2_ref, q_ref, k_ref, v_ref, gg_ref, p_ref, la_ref):
    xn = _rmsnorm(h_ref[...], g_ref[...]).astype(BF16)
    q_ref[...] = _dot(xn, w_ref[:, W0_Q:W0_Q + 512]).astype(BF16)
    k_ref[...] = _dot(xn, w_ref[:, W0_K:W0_K + 512]).astype(BF16)
    v_ref[...] = _dot(xn, w_ref[:, W0_V:W0_V + 512]).astype(BF16)
    gg_ref[...] = _dot(xn, w_ref[:, W0_G:W0_G + 512]).astype(BF16)
    p_ref[...] = _dot(xn, w_ref[:, W0_P:W0_P + 512]).astype(BF16)
    gr = _dot(xn, w_ref[:, W0_GR:W0_GR + LANE]).astype(BF16)
    z = _dot(gr, w2_ref[...]) + b2_ref[...]
    log_sig = jnp.minimum(z, 0.0) - jnp.log1p(jnp.exp(-jnp.abs(z)))
    la_ref[...] = log_sig / GLA_TAU


def _inproj0(h, gain, w0, w2, b2, *, tm=512):
    m = h.shape[0]
    wide = pl.BlockSpec((tm, 512), lambda i: (i, 0))
    return pl.pallas_call(
        _inproj0_body,
        grid=(m // tm,),
        in_specs=[
            pl.BlockSpec((tm, D_MODEL), lambda i: (i, 0)),
            pl.BlockSpec((1, D_MODEL), lambda i: (0, 0)),
            pl.BlockSpec((D_MODEL, W0_COLS), lambda i: (0, 0)),
            pl.BlockSpec((LANE, 512), lambda i: (0, 0)),
            pl.BlockSpec((1, 512), lambda i: (0, 0)),
        ],
        out_specs=[wide] * 6,
        out_shape=[jax.ShapeDtypeStruct((m, 512), BF16)] * 5 + [jax.ShapeDtypeStruct((m, 512), F32)],
        compiler_params=_cparams(("parallel",)),
        name="inproj0",
    )(h, gain, w0, w2, b2)


def _gla_body(q_ref, k_ref, v_ref, g_ref, la_ref, gn_ref, o_ref, st_ref, *, tt):
    c = GLA_CHUNK
    nchunk = tt // c

    @pl.when(pl.program_id(2) == 0)
    def _():
        st_ref[...] = jnp.zeros_like(st_ref)

    la = la_ref[0]
    la_w = jnp.concatenate([la[n * c:(n + 1) * c] for n in range(nchunk)], axis=1)
    tri = (lax.broadcasted_iota(jnp.int32, (c, c), 1) <= lax.broadcasted_iota(jnp.int32, (c, c), 0))
    tri = jnp.where(tri, 1.0, 0.0).astype(BF16)
    b_w = sum(_dot(tri, term) for term in _split3(la_w))
    b = jnp.concatenate([b_w[:, n * LANE:(n + 1) * LANE] for n in range(nchunk)], axis=0)
    b_last = jnp.concatenate([jnp.broadcast_to(b_w[c - 1:c, n * LANE:(n + 1) * LANE], (c, LANE))
                              for n in range(nchunk)], axis=0)

    q = q_ref[0].astype(F32) * (GLA_DK ** -0.5)
    k = k_ref[0].astype(F32)
    q_dec = (q * jnp.exp(b)).astype(BF16)
    k_inv = (k * jnp.exp(-b)).astype(BF16)
    k_end = (k * jnp.exp(b_last - b)).astype(BF16)
    decay = jnp.exp(b_last)
    v_bf = v_ref[0]
    v = v_bf.astype(F32)

    ci = lax.broadcasted_iota(jnp.int32, (c, c), 0)
    cj = lax.broadcasted_iota(jnp.int32, (c, c), 1)
    causal = cj <= ci

    st = st_ref[...]
    outs = []
    for n in range(nchunk):
        sl = slice(n * c, (n + 1) * c)
        att = jnp.where(causal, _dot_nt(q_dec[sl], k_inv[sl]), 0.0)
        o = _dot(att.astype(BF16), v_bf[sl]) + _dot_nt(q_dec[sl], st.astype(BF16))
        outs.append(o)
        d_state = _dot(v[sl].T.astype(BF16), k_end[sl])
        st = decay[n * c:n * c + 1, :] * st + d_state
    st_ref[...] = st

    o = jnp.concatenate(outs, axis=0)
    o = o * lax.rsqrt(jnp.mean(o * o, axis=-1, keepdims=True) + EPS) * gn_ref[...]
    g = g_ref[0].astype(F32)
    o_ref[0] = (o * (g * jax.nn.sigmoid(g))).astype(BF16)


def _gla(q, k, v, g, la, gn, *, tt=512):
    bsz, t, _ = q.shape
    blk = pl.BlockSpec((1, tt, LANE), lambda b, h, i: (b, i, h))
    return pl.pallas_call(
        functools.partial(_gla_body, tt=tt),
        grid=(bsz, GLA_HEADS, t // tt),
        in_specs=[blk, blk, blk, blk, blk, pl.BlockSpec((1, LANE), lambda b, h, i: (0, 0))],
        out_specs=blk,
        out_shape=jax.ShapeDtypeStruct((bsz, t, GLA_VAL), BF16),
        scratch_shapes=[pltpu.VMEM((GLA_DV, GLA_DK_PAD), F32)],
        compiler_params=_cparams(("parallel", "parallel", "arbitrary")),
        name="gla",
    )(q, k, v, g, la, gn)


def _outproj0_body(h_ref, oa_ref, p_ref, pw_ref, ps_ref, wo_ref, o_ref, pbuf_ref, *, tm):
    i = pl.program_id(1)

    @pl.when(i == 0)
    def _():
        pbuf_ref[0:POOL_HALO, :] = jnp.zeros((POOL_HALO, POOL_WIDTH), F32)

    p = p_ref[0].astype(F32)
    pbuf_ref[POOL_HALO:POOL_HALO + tm, :] = p
    tpos = i * tm + lax.broadcasted_iota(jnp.int32, (tm, 1), 0)
    mix = _dot(oa_ref[0], wo_ref[0:GLA_VAL, :])
    for gi, w in enumerate(POOL_WINDOWS):
        ls = slice(gi * POOL_GDIM, (gi + 1) * POOL_GDIM)
        acc = p[:, ls]
        for s in range(1, w):
            acc = acc + pbuf_ref[POOL_HALO - s:POOL_HALO - s + tm, ls]
        cnt = jnp.minimum(tpos + 1, w).astype(F32)
        pooled = acc / cnt - p[:, ls]
        ob = _dot(pooled.astype(BF16), pw_ref[gi]) * ps_ref[:, ls]
        mix = mix + _dot(ob.astype(BF16), wo_ref[GLA_VAL + gi * POOL_GDIM:GLA_VAL + (gi + 1) * POOL_GDIM, :])
    pbuf_ref[0:POOL_HALO, :] = p[tm - POOL_HALO:, :]
    o_ref[0] = h_ref[0] + mix


def _outproj0(h, oa, p, pool_w, pool_scale, w_out, *, tm=512):
    bsz, t, _ = h.shape
    return pl.pallas_call(
        functools.partial(_outproj0_body, tm=tm),
        grid=(bsz, t // tm),
        in_specs=[
            pl.BlockSpec((1, tm, D_MODEL), lambda b, i: (b, i, 0)),
            pl.BlockSpec((1, tm, GLA_VAL), lambda b, i: (b, i, 0)),
            pl.BlockSpec((1, tm, POOL_WIDTH), lambda b, i: (b, i, 0)),
            pl.BlockSpec((POOL_GROUPS, POOL_GDIM, POOL_GDIM), lambda b, i: (0, 0, 0)),
            pl.BlockSpec((1, POOL_WIDTH), lambda b, i: (0, 0)),
            pl.BlockSpec((D_MODEL, D_MODEL), lambda b, i: (0, 0)),
        ],
        out_specs=pl.BlockSpec((1, tm, D_MODEL), lambda b, i: (b, i, 0)),
        out_shape=jax.ShapeDtypeStruct((bsz, t, D_MODEL), F32),
        scratch_shapes=[pltpu.VMEM((POOL_HALO + tm, POOL_WIDTH), F32)],
        compiler_params=_cparams(("parallel", "arbitrary")),
        name="outproj0",
    )(h, oa, p, pool_w, pool_scale, w_out)


KEY_W = 2 * LANE
VAL_W = 2 * LANE
W1_Q, W1_KC, W1_VC, W1_KS, W1_VS, W1_KW, W1_VW, W1_GT = 0, 1024, 1152, 1280, 1536, 1792, 2048, 2304
W1_COLS = 2432


def _inproj1_body(h_ref, g_ref, w_ref, kf_ref, q_ref, kc_ref, vc_ref, ks_ref, vs_ref, kw_ref, vw_ref, gt_ref):
    xn = _rmsnorm(h_ref[...], g_ref[...]).astype(BF16)
    tm = xn.shape[0]
    q_ref[...] = (_dot(xn, w_ref[:, W1_Q:W1_Q + 1024]) * (NSA_HDIM ** -0.5)).astype(BF16)
    kc_ref[...] = _dot(xn, w_ref[:, W1_KC:W1_KC + NSA_KV])
    vc_ref[...] = _dot(xn, w_ref[:, W1_VC:W1_VC + NSA_KV])
    feat = kf_ref[:, 0:LANE].astype(F32)
    onehot = kf_ref[:, LANE:2 * LANE]
    ones = jnp.ones((tm, LANE), BF16)
    for g in range(NSA_KV_GROUPS):
        gl = slice(g * LANE, (g + 1) * LANE)
        ks = (_dot(xn, w_ref[:, W1_KS + g * LANE:W1_KS + (g + 1) * LANE]) + feat).astype(BF16)
        ks_ref[:, g * KEY_W:g * KEY_W + LANE] = ks
        ks_ref[:, g * KEY_W + LANE:(g + 1) * KEY_W] = onehot
        kw_ref[:, gl] = (_dot(xn, w_ref[:, W1_KW + g * LANE:W1_KW + (g + 1) * LANE]) + feat).astype(BF16)
        vs_ref[:, g * VAL_W:g * VAL_W + LANE] = _dot(xn, w_ref[:, W1_VS + g * LANE:W1_VS + (g + 1) * LANE]).astype(BF16)
        vs_ref[:, g * VAL_W + LANE:(g + 1) * VAL_W] = ones
        vw_ref[:, g * VAL_W:g * VAL_W + LANE] = _dot(xn, w_ref[:, W1_VW + g * LANE:W1_VW + (g + 1) * LANE]).astype(BF16)
        vw_ref[:, g * VAL_W + LANE:(g + 1) * VAL_W] = ones
    gt_ref[...] = jax.nn.sigmoid(_dot(xn, w_ref[:, W1_GT:W1_GT + LANE]))


def _inproj1(h, gain, w1, kfeat, *, t, tm=512):
    m = h.shape[0]
    per_seq = t // tm
    spec = lambda n: pl.BlockSpec((tm, n), lambda i: (i, 0))
    shp = lambda n, dt: jax.ShapeDtypeStruct((m, n), dt)
    g = NSA_KV_GROUPS
    return pl.pallas_call(
        _inproj1_body,
        grid=(m // tm,),
        in_specs=[
            pl.BlockSpec((tm, D_MODEL), lambda i: (i, 0)),
            pl.BlockSpec((1, D_MODEL), lambda i: (0, 0)),
            pl.BlockSpec((D_MODEL, W1_COLS), lambda i: (0, 0)),
            pl.BlockSpec((tm, KEY_W), lambda i: (i % per_seq, 0)),
        ],
        out_specs=[spec(1024), spec(NSA_KV), spec(NSA_KV), spec(g * KEY_W), spec(g * VAL_W), spec(g * LANE),
                   spec(g * VAL_W), spec(LANE)],
        out_shape=[shp(1024, BF16), shp(NSA_KV, F32), shp(NSA_KV, F32), shp(g * KEY_W, BF16), shp(g * VAL_W, BF16),
                   shp(g * LANE, BF16), shp(g * VAL_W, BF16), shp(LANE, F32)],
        compiler_params=_cparams(("parallel",)),
        name="inproj1",
    )(h, gain, w1, kfeat)


def _compress_body(z_ref, pea_ref, peb_ref, wa_ref, wb_ref, w2_ref, cf_ref, o_ref, sh_ref, *, nrow):
    z = z_ref[0]
    a = _dot((z + pea_ref[...]).astype(BF16), wa_ref[...])
    bm = _dot((z + peb_ref[...]).astype(BF16), wb_ref[...])
    sh_ref[0:nrow, :] = bm
    sh_ref[nrow:nrow + 8, :] = jnp.zeros((8, sh_ref.shape[1]), F32)
    x = a + sh_ref[1:nrow + 1, :]
    cdf = 0.5 * (1.0 + jnp.tanh(math.sqrt(2.0 / math.pi) * (x + 0.044715 * (x * x * x))))
    hid = (x * cdf).astype(BF16)
    for g in range(NSA_KV_GROUPS):
        o_ref[0, g] = (_dot(hid[:, g * CMP_HIDDEN:(g + 1) * CMP_HIDDEN], w2_ref[...]) + cf_ref[...]).astype(BF16)


def _compress(z, pea, peb, wa, wb, w2, cfeat):
    bsz, nrow, zw = z.shape
    hw = NSA_KV_GROUPS * CMP_HIDDEN
    return pl.pallas_call(
        functools.partial(_compress_body, nrow=nrow),
        grid=(bsz,),
        in_specs=[
            pl.BlockSpec((1, nrow, zw), lambda b: (b, 0, 0)),
            pl.BlockSpec((1, zw), lambda b: (0, 0)),
            pl.BlockSpec((1, zw), lambda b: (0, 0)),
            pl.BlockSpec((zw, hw), lambda b: (0, 0)),
            pl.BlockSpec((zw, hw), lambda b: (0, 0)),
            pl.BlockSpec((CMP_HIDDEN, LANE), lambda b: (0, 0)),
            pl.BlockSpec((nrow, LANE), lambda b: (0, 0)),
        ],
        out_specs=pl.BlockSpec((1, NSA_KV_GROUPS, nrow, LANE), lambda b: (b, 0, 0, 0)),
        out_shape=jax.ShapeDtypeStruct((bsz, NSA_KV_GROUPS, nrow, LANE), BF16),
        scratch_shapes=[pltpu.VMEM((nrow + 8, hw), F32)],
        compiler_params=_cparams(("parallel",)),
        name="compress",
    )(z, pea, peb, wa, wb, w2, cfeat)


def _fill_query_rows(qa_ref, q_ref, qf_ref, u=0):
    lane_lo = lax.broadcasted_iota(jnp.int32, (Q_BLOCK, LANE), 1) < NSA_HDIM
    for pr in range(NSA_PAIRS):
        even = q_ref[0, u * Q_BLOCK:(u + 1) * Q_BLOCK, pr * LANE:(pr + 1) * LANE]
        odd = pltpu.roll(even.astype(F32), NSA_HDIM, axis=1).astype(BF16)
        for hh, src in enumerate((even, odd)):
            h = 2 * pr + hh
            feat = jnp.broadcast_to(qf_ref[0, u, h:h + 1, :], (Q_BLOCK, LANE)).astype(BF16)
            qa_ref[h * Q_BLOCK:(h + 1) * Q_BLOCK, 0:LANE] = jnp.where(lane_lo, src, feat)


def _merge_pairs(o_ref, o, u=0):
    lane_lo = lax.broadcasted_iota(jnp.int32, (Q_BLOCK, LANE), 1) < NSA_HDIM
    for pr in range(NSA_PAIRS):
        a = o[(2 * pr) * Q_BLOCK:(2 * pr + 1) * Q_BLOCK]
        b = o[(2 * pr + 1) * Q_BLOCK:(2 * pr + 2) * Q_BLOCK]
        o_ref[0, u * Q_BLOCK:(u + 1) * Q_BLOCK, pr * LANE:(pr + 1) * LANE] = jnp.where(lane_lo, a, b).astype(o_ref.dtype)


def _head_tile(bias):
    return jnp.concatenate([bias] * NSA_HPG, axis=0)


SEL_TILES = 4


def _compressed_branch(u, qf_ref, q_ref, kc_ref, vc_ref, mcst_ref, oc_ref, qa_ref):
    t0 = (pl.program_id(2) * SEL_TILES + u) * Q_BLOCK
    ncmp_pad = kc_ref.shape[2]
    nq = Q_BLOCK
    _fill_query_rows(qa_ref, q_ref, qf_ref, u)
    tpos = t0 + lax.broadcasted_iota(jnp.int32, (nq, 1), 0)
    cmp_end = lax.broadcasted_iota(jnp.int32, (1, ncmp_pad), 1) * CMP_STRIDE + (CMP_BLOCK - 1)
    bias_c = jnp.where(tpos >= cmp_end, 0.0, NEG)
    any_c = jnp.where(tpos >= (CMP_BLOCK - 1), 1.0, 0.0)
    s = _dot_nt(qa_ref[...], kc_ref[0, 0]) + _head_tile(bias_c)
    e = jnp.exp(s - jnp.max(s, axis=-1, keepdims=True))
    p = e * (_head_tile(any_c) / jnp.sum(e, axis=-1, keepdims=True))
    _merge_pairs(oc_ref, _dot(p.astype(BF16), vc_ref[0, 0]), u)
    p_sum = p[0:nq]
    for h in range(1, NSA_HPG):
        p_sum = p_sum + p[h * nq:(h + 1) * nq]

    return sum(_dot_nt(mcst_ref[...], term) for term in _split3(p_sum))


def _nsa_select_body(qf_ref, q_ref, kc_ref, vc_ref, mcst_ref, oc_ref, sb_ref, lst_ref, *qa_refs):
    nt = SEL_TILES
    nq = Q_BLOCK
    wide = nt * nq
    imp = jnp.concatenate([_compressed_branch(u, qf_ref, q_ref, kc_ref, vc_ref, mcst_ref, oc_ref, qa_refs[u])
                           for u in range(nt)], axis=1)

    blk = lax.broadcasted_iota(jnp.int32, (MAX_SLC, wide), 0)
    tpos_l = pl.program_id(2) * wide + lax.broadcasted_iota(jnp.int32, (MAX_SLC, wide), 1)
    cur = tpos_l >> SLC_SHIFT
    forced = (blk == 0) | (blk == cur) | (blk == cur - 1)
    valid_b = blk * SLC_BLOCK <= tpos_l
    score = jnp.where(valid_b, jnp.where(forced, FORCE, imp), -1.0)
    blk_f = blk.astype(F32)
    sel = jnp.zeros((MAX_SLC, wide), F32)
    for _ in range(SLC_TOP):
        mx = jnp.max(score, axis=0, keepdims=True)
        first = jnp.min(jnp.where(score == mx, blk_f, float(MAX_SLC)), axis=0, keepdims=True)
        pick = blk_f == first
        sel = jnp.where(pick, 1.0, sel)
        score = jnp.where(pick, -2.0, score)
    own = (tpos_l >> (SLC_SHIFT + 1)) << 1
    sel = jnp.where(valid_b & (blk < own), sel, 0.0)
    for u in range(nt):
        sel_u = sel[:, u * nq:(u + 1) * nq]
        sb_ref[0, 0, u * nq:(u + 1) * nq, :] = jnp.where(sel_u.T > 0.5, 0.0, NEG).astype(BF16)

    used_b = jnp.concatenate(
        [jnp.broadcast_to(jnp.max(sel[:, u * nq:(u + 1) * nq], axis=1, keepdims=True), (MAX_SLC, LANE))
         for u in range(nt)], axis=1).astype(BF16)
    r = lax.broadcasted_iota(jnp.int32, (MAX_SLC, LANE), 0)
    c = lax.broadcasted_iota(jnp.int32, (MAX_SLC, LANE), 1)
    before = jnp.where(c < r, 1.0, 0.0).astype(BF16)
    pos = _dot(before, used_b)
    slot_w = (lax.broadcasted_iota(jnp.int32, (MAX_SLC, nt * LANE), 1) & (LANE - 1)).astype(F32)
    place = jnp.where((pos == slot_w) & (used_b > 0.5), 1.0, 0.0).astype(BF16)
    ids = lax.broadcasted_iota(jnp.int32, (8, LANE), 1).astype(F32).astype(BF16)
    lst = _dot(ids, place)
    cnt = _dot(jnp.ones((8, LANE), BF16), used_b)
    lane8 = lax.broadcasted_iota(jnp.int32, (8, nt * LANE), 1)
    filler = ((pl.program_id(2) * nt + (lane8 >> 7)) << 1).astype(F32)
    lst = jnp.where((lane8 & (LANE - 1)).astype(F32) < cnt, lst, filler)
    row0 = lax.broadcasted_iota(jnp.int32, (8, nt * LANE), 0) == 0
    out = jnp.where(row0, lst, cnt).astype(jnp.int32)
    for u in range(nt):
        lst_ref[0, 0, u] = out[:, u * LANE:(u + 1) * LANE]


def _nsa_select(qfeat, q, kcmp, vcmp, mcst):
    bsz, t, _ = q.shape
    ncmp_pad = kcmp.shape[2]
    nqt = t // Q_BLOCK
    gw = NSA_HPG * NSA_HDIM
    assert nqt % SEL_TILES == 0
    nt = SEL_TILES
    qspec = pl.BlockSpec((1, nt * Q_BLOCK, gw), lambda b, g, i: (b, i, g))
    cspec = pl.BlockSpec((1, 1, ncmp_pad, LANE), lambda b, g, i: (b, g, 0, 0))
    return pl.pallas_call(
        _nsa_select_body,
        grid=(bsz, NSA_KV_GROUPS, nqt // nt),
        in_specs=[
            pl.BlockSpec((1, nt, NSA_HPG, LANE), lambda b, g, i: (g, i, 0, 0)),
            qspec, cspec, cspec,
            pl.BlockSpec((MAX_SLC, ncmp_pad), lambda b, g, i: (0, 0)),
        ],
        out_specs=[qspec,
                   pl.BlockSpec((1, 1, nt * Q_BLOCK, MAX_SLC), lambda b, g, i: (b, g, i, 0)),
                   pl.BlockSpec((1, 1, nt, 8, LANE), lambda b, g, i: (b, g, i, 0, 0))],
        out_shape=[jax.ShapeDtypeStruct((bsz, t, NSA_HEADS * NSA_HDIM), BF16),
                   jax.ShapeDtypeStruct((bsz, NSA_KV_GROUPS, t, MAX_SLC), BF16),
                   jax.ShapeDtypeStruct((bsz, NSA_KV_GROUPS, nqt, 8, LANE), jnp.int32)],
        scratch_shapes=[pltpu.VMEM((ROWS, LANE), BF16)] * nt,
        compiler_params=_cparams(("parallel", "parallel", "parallel")),
        name="nsa_select",
    )(qfeat, q, kcmp, vcmp, mcst)


def _nsa_attend_body(lst_ref, cnt_ref, qf_ref, q_ref, sb_ref, ks_ref, vs_ref, kw_ref, vw_ref, os_ref, ow_ref,
                     qa_ref, kst0_ref, kst1_ref, vst0_ref, vst1_ref, s0_ref, s1_ref, m_ref, acc_ref):
    b = pl.program_id(0)
    g = pl.program_id(1)
    qi = pl.program_id(2)
    nqt = pl.num_programs(2)
    t0 = pl.multiple_of(qi * Q_BLOCK, Q_BLOCK)
    nq = Q_BLOCK
    _fill_query_rows(qa_ref, q_ref, qf_ref)
    sb = sb_ref[0, 0]
    for h in range(NSA_HPG):
        qa_ref[h * nq:(h + 1) * nq, LANE:2 * LANE] = sb
    qa_lo = qa_ref[:, 0:LANE]

    tile = (b * NSA_KV_GROUPS + g) * nqt + qi
    base = tile * LANE
    count = cnt_ref[tile]

    bufs = ((kst0_ref, vst0_ref, s0_ref), (kst1_ref, vst1_ref, s1_ref))

    def stage_and_score(c, slot):
        kst_ref, vst_ref, s_ref = bufs[slot]
        for j in range(STAGE_BLOCKS):
            idx = jnp.minimum(c * STAGE_BLOCKS + j, LANE - 1)
            r0 = pl.multiple_of(lst_ref[base + idx] * SLC_BLOCK, SLC_BLOCK)
            kst_ref[j * SLC_BLOCK:(j + 1) * SLC_BLOCK, :] = ks_ref[0, pl.ds(r0, SLC_BLOCK), :]
            vst_ref[j * SLC_BLOCK:(j + 1) * SLC_BLOCK, :] = vs_ref[0, pl.ds(r0, SLC_BLOCK), :]
        s_ref[...] = _dot_nt(qa_ref[...], kst_ref[...])

    def softmax_pv(slot):
        _, vst_ref, s_ref = bufs[slot]
        s = s_ref[...]
        m_old = m_ref[...]
        m_new = jnp.maximum(m_old, jnp.max(s, axis=-1, keepdims=True))
        acc_ref[...] = jnp.exp(m_old - m_new) * acc_ref[...] + _dot(jnp.exp(s - m_new).astype(BF16), vst_ref[...])
        m_ref[...] = m_new

    wk = WINDOW + nq
    w0 = pl.multiple_of(jnp.maximum(t0 - WINDOW, 0), nq)
    dist = (t0 + lax.broadcasted_iota(jnp.int32, (nq, 1), 0)) - (w0 + lax.broadcasted_iota(jnp.int32, (1, wk), 1))
    bias_w = jnp.where((dist >= 0) & (dist < WINDOW), 0.0, NEG)
    s = _dot_nt(qa_lo, kw_ref[0, pl.ds(w0, wk), :]) + _head_tile(bias_w)
    e = jnp.exp(s - jnp.max(s, axis=-1, keepdims=True))
    acc = _dot(e.astype(BF16), vw_ref[0, pl.ds(w0, wk), :])
    _merge_pairs(ow_ref, acc[:, 0:LANE] / acc[:, LANE:2 * LANE])

    ti = lax.broadcasted_iota(jnp.int32, (nq, nq), 0)
    tj = lax.broadcasted_iota(jnp.int32, (nq, nq), 1)
    s = _dot_nt(qa_lo, ks_ref[0, pl.ds(t0, nq), 0:LANE]) + _head_tile(jnp.where(tj <= ti, 0.0, NEG))
    m = jnp.max(s, axis=-1, keepdims=True)
    m_ref[...] = m
    acc_ref[...] = _dot(jnp.exp(s - m).astype(BF16), vs_ref[0, pl.ds(t0, nq), :])

    stage_and_score(0, 0)

    n_groups = (count + STAGE_BLOCKS - 1) // STAGE_BLOCKS

    def step(c, carry):
        stage_and_score(2 * c + 1, 1)
        softmax_pv(0)

        @pl.when(2 * c + 1 < n_groups)
        def _():
            stage_and_score(2 * c + 2, 0)
            softmax_pv(1)

        return carry

    lax.fori_loop(0, (n_groups + 1) // 2, step, 0)
    acc = acc_ref[...]
    _merge_pairs(os_ref, acc[:, 0:LANE] / acc[:, LANE:2 * LANE])


def _nsa_attend(lists, counts, qfeat, q, selbias, ks, vs, kw, vw):
    bsz, t, _ = q.shape
    nqt = t // Q_BLOCK
    gw = NSA_HPG * NSA_HDIM
    qspec = pl.BlockSpec((1, Q_BLOCK, gw), lambda b, g, i, lst, cnt: (b, i, g))
    out = jax.ShapeDtypeStruct((bsz, t, NSA_HEADS * NSA_HDIM), BF16)
    stage = STAGE_BLOCKS * SLC_BLOCK
    return pl.pallas_call(
        _nsa_attend_body,
        grid_spec=pltpu.PrefetchScalarGridSpec(
            num_scalar_prefetch=2,
            grid=(bsz, NSA_KV_GROUPS, nqt),
            in_specs=[
                pl.BlockSpec((1, 1, NSA_HPG, LANE), lambda b, g, i, lst, cnt: (g, i, 0, 0)),
                qspec,
                pl.BlockSpec((1, 1, Q_BLOCK, MAX_SLC), lambda b, g, i, lst, cnt: (b, g, i, 0)),
                pl.BlockSpec((1, t, KEY_W), lambda b, g, i, lst, cnt: (b, 0, g)),
                pl.BlockSpec((1, t, VAL_W), lambda b, g, i, lst, cnt: (b, 0, g)),
                pl.BlockSpec((1, t, LANE), lambda b, g, i, lst, cnt: (b, 0, g)),
                pl.BlockSpec((1, t, VAL_W), lambda b, g, i, lst, cnt: (b, 0, g)),
            ],
            out_specs=[qspec, qspec],
            scratch_shapes=[
                pltpu.VMEM((ROWS, KEY_W), BF16),
                pltpu.VMEM((stage, KEY_W), BF16),
                pltpu.VMEM((stage, KEY_W), BF16),
                pltpu.VMEM((stage, VAL_W), BF16),
                pltpu.VMEM((stage, VAL_W), BF16),
                pltpu.VMEM((ROWS, stage), F32),
                pltpu.VMEM((ROWS, stage), F32),
                pltpu.VMEM((ROWS, 1), F32),
                pltpu.VMEM((ROWS, VAL_W), F32),
            ],
        ),
        out_shape=[out, out],
        compiler_params=_cparams(("parallel", "parallel", "arbitrary")),
        name="nsa_attend",
    )(lists, counts, qfeat, q, selbias, ks, vs, kw, vw)


def _outproj1_body(h_ref, oc_ref, os_ref, ow_ref, gt_ref, ex_ref, wo_ref, o_ref):
    gt = gt_ref[...]
    hi = gt.astype(BF16)
    lo = (gt - hi.astype(F32)).astype(BF16)
    o = jnp.zeros(oc_ref.shape, F32)
    for c, br in enumerate((oc_ref, os_ref, ow_ref)):
        ge = _dot(hi, ex_ref[c]) + _dot(lo, ex_ref[c])
        o = o + ge * br[...].astype(F32)
    o_ref[...] = h_ref[...] + _dot(o.astype(BF16), wo_ref[...])


def _outproj1(h, oc, os_, ow, gates, expand, w_out, *, tm=512):
    m = h.shape[0]
    row = pl.BlockSpec((tm, D_MODEL), lambda i: (i, 0))
    return pl.pallas_call(
        _outproj1_body,
        grid=(m // tm,),
        in_specs=[row, row, row, row,
                  pl.BlockSpec((tm, LANE), lambda i: (i, 0)),
                  pl.BlockSpec((3, LANE, D_MODEL), lambda i: (0, 0, 0)),
                  pl.BlockSpec((D_MODEL, D_MODEL), lambda i: (0, 0))],
        out_specs=row,
        out_shape=jax.ShapeDtypeStruct((m, D_MODEL), F32),
        compiler_params=_cparams(("parallel",)),
        name="outproj1",
    )(h, oc, os_, ow, gates, expand, w_out)


def _pad_heads(w, heads, width, padded):
    lead = w.shape[:-1]
    w = w.reshape(lead + (heads, width))
    w = jnp.pad(w, [(0, 0)] * len(lead) + [(0, 0), (0, padded - width)])
    return w.reshape(lead + (heads * padded,))


def _dup_groups(w):
    lead = w.shape[:-1]
    w = w.reshape(lead + (NSA_KV_GROUPS, NSA_HDIM))
    return jnp.concatenate([w, w], axis=-1).reshape(lead + (NSA_KV_GROUPS * LANE,))


def _np_split3(x):
    out = []
    r = np.asarray(x, np.float64)
    for _ in range(3):
        part = np.asarray(np.asarray(r, np.float32).astype(jnp.bfloat16), np.float64)
        out.append(part.astype(np.float32))
        r = r - part
    return out


def _position_terms(pos):
    pos = np.asarray(pos)
    f = np.zeros((pos.shape[0], LANE), np.float32)
    f[:, FEAT0 + 0:FEAT0 + 3] = (pos >> SLC_SHIFT)[:, None]
    f[:, FEAT0 + 3:FEAT0 + 6] = (pos & (SLC_BLOCK - 1))[:, None]
    f[:, FEAT0 + 6:FEAT0 + 9] = 1.0
    return f


def _slope_terms(nqt):
    slopes = np.array([2.0 ** (-8.0 * (i + 1) / NSA_HEADS) for i in range(NSA_HEADS)], dtype=np.float32)
    slopes = slopes.reshape(NSA_HPG, NSA_KV_GROUPS).T.astype(np.float64)
    f = np.zeros((NSA_KV_GROUPS, nqt, NSA_HPG, LANE), np.float32)
    t0 = (np.arange(nqt) * Q_BLOCK).astype(np.float64)
    for k, part in enumerate(_np_split3(slopes * SLC_BLOCK)):
        f[:, :, :, FEAT0 + k] = part[:, None, :]
    for k, part in enumerate(_np_split3(slopes)):
        f[:, :, :, FEAT0 + 3 + k] = part[:, None, :]
    for k, part in enumerate(_np_split3(-slopes[:, None, :] * t0[None, :, None])):
        f[:, :, :, FEAT0 + 6 + k] = part
    return f


def _key_terms(t):
    pos = np.arange(t)
    onehot = np.zeros((t, MAX_SLC), np.float32)
    onehot[pos, pos >> SLC_SHIFT] = 1.0
    return np.concatenate([_position_terms(pos), onehot], axis=1)


def _cmp_to_slc_t(ncmp_pad):
    cs = np.arange(ncmp_pad) * CMP_STRIDE
    ss = np.arange(MAX_SLC) * SLC_BLOCK
    ov = np.minimum(cs[None] + CMP_BLOCK, ss[:, None] + SLC_BLOCK) - np.maximum(cs[None], ss[:, None])
    return (np.clip(ov, 0, None) / CMP_BLOCK).astype(np.float32)


def _gate_expand():
    e = np.zeros((3, LANE, NSA_HEADS * NSA_HDIM), np.float32)
    for c in range(3):
        for g in range(NSA_KV_GROUPS):
            for h in range(NSA_HPG):
                col = g * NSA_HPG * 3 + h * 3 + c
                base = (g * NSA_HPG + h) * NSA_HDIM
                e[c, col, base:base + NSA_HDIM] = 1.0
    return e


def _compress_weights(w1, w2, pe, dup):
    half = CMP_BLOCK // 2
    eye = jnp.eye(NSA_KV_GROUPS, dtype=w1.dtype)
    w1 = w1.reshape(2, half, NSA_HDIM, CMP_HIDDEN)
    wexp = jnp.einsum('sjdc,gh->sjgdhc', w1, eye).reshape(2, half * NSA_KV, NSA_KV_GROUPS * CMP_HIDDEN)
    pe = jnp.broadcast_to(pe.reshape(2, half, 1, NSA_HDIM), (2, half, NSA_KV_GROUPS, NSA_HDIM)).reshape(2, 1, half * NSA_KV)
    w2w = jnp.concatenate([w2, w2 if dup else jnp.zeros_like(w2)], axis=-1)
    return pe[0], pe[1], wexp[0].astype(BF16), wexp[1].astype(BF16), w2w.astype(BF16)


def _layer0_mixer(h3, norm_mix, a_w_in, a_gate_w2, a_gate_b, a_gla_norm, a_pool_w, a_pool_scale, a_w_out):
    bsz, t, d = h3.shape
    cuts = np.cumsum([GLA_KEY, GLA_KEY, GLA_VAL, GLA_VAL, GLA_GATE_RANK]).tolist()
    wq, wk, wv, wg, wgr, wp = jnp.split(a_w_in, cuts, axis=-1)
    w0 = jnp.concatenate([
        _pad_heads(wq, GLA_HEADS, GLA_DK, GLA_DK_PAD), _pad_heads(wk, GLA_HEADS, GLA_DK, GLA_DK_PAD),
        wv, wg, wp, jnp.pad(wgr, ((0, 0), (0, LANE - GLA_GATE_RANK)))], axis=-1).astype(BF16)
    w2 = jnp.pad(_pad_heads(a_gate_w2, GLA_HEADS, GLA_DK, GLA_DK_PAD), ((0, LANE - GLA_GATE_RANK), (0, 0))).astype(BF16)
    b2 = _pad_heads(a_gate_b, GLA_HEADS, GLA_DK, GLA_DK_PAD).reshape(1, -1)
    q, k, v, g, p, la = _inproj0(h3.reshape(bsz * t, d), norm_mix.reshape(1, d), w0, w2, b2)
    r3 = lambda z: z.reshape(bsz, t, -1)
    oa = _gla(r3(q), r3(k), r3(v), r3(g), r3(la), a_gla_norm.reshape(1, GLA_DV))
    return _outproj0(h3, oa, r3(p), a_pool_w.astype(BF16), a_pool_scale.reshape(1, -1), a_w_out.astype(BF16))


def _layer1_mixer(h3, norm_mix, c_w_in, c_cmp_pe, c_cmpk_w1, c_cmpk_w2, c_cmpv_w1, c_cmpv_w2, c_w_out):
    bsz, t, d = h3.shape
    assert t // SLC_BLOCK <= MAX_SLC and t >= WINDOW + Q_BLOCK
    nqt = t // Q_BLOCK
    cuts = (NSA_HEADS * NSA_HDIM + NSA_KV * np.arange(7)).tolist()
    wq, wkc, wvc, wks, wvs, wkw, wvw, wgt = jnp.split(c_w_in, cuts, axis=-1)
    w1 = jnp.concatenate([wq, wkc, wvc, _pad_heads(wks, NSA_KV_GROUPS, NSA_HDIM, LANE), _dup_groups(wvs),
                          _pad_heads(wkw, NSA_KV_GROUPS, NSA_HDIM, LANE), _dup_groups(wvw),
                          jnp.pad(wgt, ((0, 0), (0, LANE - wgt.shape[-1])))], axis=-1).astype(BF16)
    kfeat = jnp.asarray(_key_terms(t)).astype(BF16)
    q, kc, vc, ks, vs, kw, vw, gates = _inproj1(h3.reshape(bsz * t, d), norm_mix.reshape(1, d), w1, kfeat, t=t)
    nrow = t // CMP_STRIDE
    cfeat = jnp.asarray(_position_terms(np.arange(nrow) * CMP_STRIDE + CMP_BLOCK - 1))
    zero = jnp.zeros_like(cfeat)
    kcmp = _compress(kc.reshape(bsz, nrow, CMP_STRIDE * NSA_KV),
                     *_compress_weights(c_cmpk_w1, c_cmpk_w2, c_cmp_pe, dup=False), cfeat)
    vcmp = _compress(vc.reshape(bsz, nrow, CMP_STRIDE * NSA_KV),
                     *_compress_weights(c_cmpv_w1, c_cmpv_w2, c_cmp_pe, dup=True), zero)
    r3 = lambda z: z.reshape(bsz, t, -1)
    qfeat = jnp.asarray(_slope_terms(nqt))
    mcst = jnp.asarray(_cmp_to_slc_t(nrow)).astype(BF16)
    oc, selbias, lists = _nsa_select(qfeat, r3(q), kcmp, vcmp, mcst)
    os_, ow = _nsa_attend(lists[:, :, :, 0, :].reshape(-1), lists[:, :, :, 1, 0].reshape(-1), qfeat, r3(q), selbias,
                          r3(ks), r3(vs), r3(kw), r3(vw))
    f2 = lambda z: z.reshape(bsz * t, -1)
    out = _outproj1(h3.reshape(bsz * t, d), f2(oc), f2(os_), f2(ow), gates,
                    jnp.asarray(_gate_expand()).astype(BF16), c_w_out.astype(BF16))
    return out.reshape(bsz, t, d)


def kernel(x, norm_ffn1, ffn1_wg, ffn1_wu, ffn1_wd, norm_mix, norm_ffn2, ffn2_wg, ffn2_wu, ffn2_wd, a_w_in, a_gate_w2, a_gate_b, a_gla_norm, a_pool_w, a_pool_scale, a_w_out, c_w_in, c_cmp_pe, c_cmpk_w1, c_cmpk_w2, c_cmpv_w1, c_cmpv_w2, c_w_out, final_norm):
    bsz, t, d = x.shape
    depth = norm_ffn1.shape[0]
    fg = final_norm.reshape(1, d)
    h = x
    for l in range(depth):
        h = _ffn(h.reshape(bsz * t, d), norm_ffn1[l].reshape(1, d), ffn1_wg[l], ffn1_wu[l], ffn1_wd[l], fg,
                 final_norm=False).reshape(bsz, t, d)
        i = l // 2
        if l % 2 == 0:
            h = _layer0_mixer(h, norm_mix[l], a_w_in[i], a_gate_w2[i], a_gate_b[i], a_gla_norm[i],
                              a_pool_w[i], a_pool_scale[i], a_w_out[i])
        else:
            h = _layer1_mixer(h, norm_mix[l], c_w_in[i], c_cmp_pe[i], c_cmpk_w1[i], c_cmpk_w2[i],
                              c_cmpv_w1[i], c_cmpv_w2[i], c_w_out[i])
        h = _ffn(h.reshape(bsz * t, d), norm_ffn2[l].reshape(1, d), ffn2_wg[l], ffn2_wu[l], ffn2_wd[l], fg,
                 final_norm=(l == depth - 1)).reshape(bsz, t, d)
    return h
```

```python
import functools
import math

import numpy as np
import jax
import jax.numpy as jnp
from jax import lax
from jax.experimental import pallas as pl
from jax.experimental.pallas import tpu as pltpu

F32 = jnp.float32
BF16 = jnp.bfloat16

D_MODEL = 1024
D_FF = 2816
EPS = 1e-6

GLA_HEADS = 4
GLA_DV = 128
GLA_DK = 64
GLA_DK_PAD = 128
GLA_KEY = GLA_HEADS * GLA_DK
GLA_VAL = GLA_HEADS * GLA_DV
GLA_GATE_RANK = 16
GLA_TAU = 16.0
GLA_CHUNK = 64

POOL_WINDOWS = (2, 4, 8, 16)
POOL_GROUPS = 4
POOL_GDIM = 128
POOL_WIDTH = POOL_GROUPS * POOL_GDIM
POOL_HALO = 16

NSA_HDIM = 64
NSA_HEADS = 16
NSA_KV_GROUPS = 2
NSA_HPG = NSA_HEADS // NSA_KV_GROUPS
NSA_PAIRS = NSA_HPG // 2
NSA_KV = NSA_KV_GROUPS * NSA_HDIM
CMP_BLOCK = 32
CMP_STRIDE = 16
CMP_HIDDEN = 256
SLC_BLOCK = 64
SLC_SHIFT = 6
SLC_TOP = 16
WINDOW = 512
Q_BLOCK = 128
NEG = -1e30
FORCE = 1e4

LANE = 128
VMEM_LIMIT = 56 * 1024 * 1024

MAX_SLC = LANE
FEAT0 = NSA_HDIM
STAGE_BLOCKS = 8
ROWS = NSA_HPG * Q_BLOCK


def _cparams(sem):
    return pltpu.CompilerParams(dimension_semantics=sem, vmem_limit_bytes=VMEM_LIMIT)


def _rmsnorm(x, g):
    return x * lax.rsqrt(jnp.mean(x * x, axis=-1, keepdims=True) + EPS) * g


def _dot(a, b):
    return jnp.dot(a, b, preferred_element_type=F32)


def _dot_nt(a, b):
    return lax.dot_general(a, b, (((1,), (1,)), ((), ())), preferred_element_type=F32)


def _split3(x):
    hi = x.astype(BF16)
    r1 = x - hi.astype(F32)
    mid = r1.astype(BF16)
    lo = (r1 - mid.astype(F32)).astype(BF16)
    return hi, mid, lo


def _ffn_body(h_ref, g_ref, wg_ref, wu_ref, wd_ref, fg_ref, o_ref, xn_ref, acc_ref, *, final_norm):
    nchunk = wg_ref.shape[0]

    def down(xn, j):
        a = _dot(xn, wg_ref[j])
        u = _dot(xn, wu_ref[j])
        hid = (a * jax.nn.sigmoid(a) * u).astype(BF16)
        return _dot(hid, wd_ref[j])

    xn = _rmsnorm(h_ref[...], g_ref[...]).astype(BF16)
    xn_ref[...] = xn
    acc_ref[...] = down(xn, 0)

    def step(j, carry):
        acc_ref[...] += down(xn_ref[...], j)
        return carry

    lax.fori_loop(1, nchunk - 1, step, 0)
    y = h_ref[...] + 0.5 * (acc_ref[...] + down(xn_ref[...], nchunk - 1))
    if final_norm:
        y = _rmsnorm(y, fg_ref[...])
    o_ref[...] = y


def _ffn(h, gain, wg, wu, wd, final_gain, *, final_norm, tm=1024, tf=256):
    m = h.shape[0]
    nchunk = D_FF // tf
    assert m % tm == 0 and D_FF % tf == 0 and nchunk >= 2
    resident = lambda shape: pl.BlockSpec(shape, lambda i: (0,) * len(shape), pipeline_mode=pl.Buffered(1))
    return pl.pallas_call(
        functools.partial(_ffn_body, final_norm=final_norm),
        grid=(m // tm,),
        in_specs=[
            pl.BlockSpec((tm, D_MODEL), lambda i: (i, 0)),
            pl.BlockSpec((1, D_MODEL), lambda i: (0, 0)),
            resident((nchunk, D_MODEL, tf)),
            resident((nchunk, D_MODEL, tf)),
            resident((nchunk, tf, D_MODEL)),
            pl.BlockSpec((1, D_MODEL), lambda i: (0, 0)),
        ],
        out_specs=pl.BlockSpec((tm, D_MODEL), lambda i: (i, 0)),
        out_shape=jax.ShapeDtypeStruct((m, D_MODEL), F32),
        scratch_shapes=[pltpu.VMEM((tm, D_MODEL), BF16), pltpu.VMEM((tm, D_MODEL), F32)],
        compiler_params=_cparams(("parallel",)),
        name="ffn",
    )(h, gain, _chunk_cols(wg, tf), _chunk_cols(wu, tf), wd.astype(BF16).reshape(nchunk, tf, D_MODEL), final_gain)


def _chunk_cols(w, tf):
    d, f = w.shape
    return w.astype(BF16).reshape(d, f // tf, tf).transpose(1, 0, 2)


W0_Q, W0_K, W0_V, W0_G, W0_P, W0_GR = 0, 512, 1024, 1536, 2048, 2560
W0_COLS = 2688


def _inproj0_body(h_ref, g_ref, w_ref, w2_ref, b2_ref, q_ref, k_ref, v_ref, gg_ref, p_ref, la_ref):
    xn = _rmsnorm(h_ref[...], g_ref[...]).astype(BF16)
    q_ref[...] = _dot(xn, w_ref[:, W0_Q:W0_Q + 512]).astype(BF16)
    k_ref[...] = _dot(xn, w_ref[:, W0_K:W0_K + 512]).astype(BF16)
    v_ref[...] = _dot(xn, w_ref[:, W0_V:W0_V + 512]).astype(BF16)
    gg_ref[...] = _dot(xn, w_ref[:, W0_G:W0_G + 512]).astype(BF16)
    p_ref[...] = _dot(xn, w_ref[:, W0_P:W0_P + 512]).astype(BF16)
    gr = _dot(xn, w_ref[:, W0_GR:W0_GR + LANE]).astype(BF16)
    z = _dot(gr, w2_ref[...]) + b2_ref[...]
    log_sig = jnp.minimum(z, 0.0) - jnp.log1p(jnp.exp(-jnp.abs(z)))
    la_ref[...] = log_sig / GLA_TAU


def _inproj0(h, gain, w0, w2, b2, *, tm=512):
    m = h.shape[0]
    wide = pl.BlockSpec((tm, 512), lambda i: (i, 0))
    return pl.pallas_call(
        _inproj0_body,
        grid=(m // tm,),
        in_specs=[
            pl.BlockSpec((tm, D_MODEL), lambda i: (i, 0)),
            pl.BlockSpec((1, D_MODEL), lambda i: (0, 0)),
            pl.BlockSpec((D_MODEL, W0_COLS), lambda i: (0, 0)),
            pl.BlockSpec((LANE, 512), lambda i: (0, 0)),
            pl.BlockSpec((1, 512), lambda i: (0, 0)),
        ],
        out_specs=[wide] * 6,
        out_shape=[jax.ShapeDtypeStruct((m, 512), BF16)] * 5 + [jax.ShapeDtypeStruct((m, 512), F32)],
        compiler_params=_cparams(("parallel",)),
        name="inproj0",
    )(h, gain, w0, w2, b2)


def _gla_body(q_ref, k_ref, v_ref, g_ref, la_ref, gn_ref, o_ref, st_ref, *, tt):
    c = GLA_CHUNK
    nchunk = tt // c

    @pl.when(pl.program_id(2) == 0)
    def _():
        st_ref[...] = jnp.zeros_like(st_ref)

    la = la_ref[0]
    la_w = jnp.concatenate([la[n * c:(n + 1) * c] for n in range(nchunk)], axis=1)
    tri = (lax.broadcasted_iota(jnp.int32, (c, c), 1) <= lax.broadcasted_iota(jnp.int32, (c, c), 0))
    tri = jnp.where(tri, 1.0, 0.0).astype(BF16)
    b_w = sum(_dot(tri, term) for term in _split3(la_w))
    b = jnp.concatenate([b_w[:, n * LANE:(n + 1) * LANE] for n in range(nchunk)], axis=0)
    b_last = jnp.concatenate([jnp.broadcast_to(b_w[c - 1:c, n * LANE:(n + 1) * LANE], (c, LANE))
                              for n in range(nchunk)], axis=0)

    q = q_ref[0].astype(F32) * (GLA_DK ** -0.5)
    k = k_ref[0].astype(F32)
    q_dec = (q * jnp.exp(b)).astype(BF16)
    k_inv = (k * jnp.exp(-b)).astype(BF16)
    k_end = (k * jnp.exp(b_last - b)).astype(BF16)
    decay = jnp.exp(b_last)
    v_bf = v_ref[0]
    v = v_bf.astype(F32)

    ci = lax.broadcasted_iota(jnp.int32, (c, c), 0)
    cj = lax.broadcasted_iota(jnp.int32, (c, c), 1)
    causal = cj <= ci

    st = st_ref[...]
    outs = []
    for n in range(nchunk):
        sl = slice(n * c, (n + 1) * c)
        att = jnp.where(causal, _dot_nt(q_dec[sl], k_inv[sl]), 0.0)
        o = _dot(att.astype(BF16), v_bf[sl]) + _dot_nt(q_dec[sl], st.astype(BF16))
        outs.append(o)
        d_state = _dot(v[sl].T.astype(BF16), k_end[sl])
        st = decay[n * c:n * c + 1, :] * st + d_state
    st_ref[...] = st

    o = jnp.concatenate(outs, axis=0)
    o = o * lax.rsqrt(jnp.mean(o * o, axis=-1, keepdims=True) + EPS) * gn_ref[...]
    g = g_ref[0].astype(F32)
    o_ref[0] = (o * (g * jax.nn.sigmoid(g))).astype(BF16)


def _gla(q, k, v, g, la, gn, *, tt=512):
    bsz, t, _ = q.shape
    blk = pl.BlockSpec((1, tt, LANE), lambda b, h, i: (b, i, h))
    return pl.pallas_call(
        functools.partial(_gla_body, tt=tt),
        grid=(bsz, GLA_HEADS, t // tt),
        in_specs=[blk, blk, blk, blk, blk, pl.BlockSpec((1, LANE), lambda b, h, i: (0, 0))],
        out_specs=blk,
        out_shape=jax.ShapeDtypeStruct((bsz, t, GLA_VAL), BF16),
        scratch_shapes=[pltpu.VMEM((GLA_DV, GLA_DK_PAD), F32)],
        compiler_params=_cparams(("parallel", "parallel", "arbitrary")),
        name="gla",
    )(q, k, v, g, la, gn)


def _outproj0_body(h_ref, oa_ref, p_ref, pw_ref, ps_ref, wo_ref, o_ref, pbuf_ref, *, tm):
    i = pl.program_id(1)

    @pl.when(i == 0)
    def _():
        pbuf_ref[0:POOL_HALO, :] = jnp.zeros((POOL_HALO, POOL_WIDTH), F32)

    p = p_ref[0].astype(F32)
    pbuf_ref[POOL_HALO:POOL_HALO + tm, :] = p
    tpos = i * tm + lax.broadcasted_iota(jnp.int32, (tm, 1), 0)
    mix = _dot(oa_ref[0], wo_ref[0:GLA_VAL, :])
    for gi, w in enumerate(POOL_WINDOWS):
        ls = slice(gi * POOL_GDIM, (gi + 1) * POOL_GDIM)
        acc = p[:, ls]
        for s in range(1, w):
            acc = acc + pbuf_ref[POOL_HALO - s:POOL_HALO - s + tm, ls]
        cnt = jnp.minimum(tpos + 1, w).astype(F32)
        pooled = acc / cnt - p[:, ls]
        ob = _dot(pooled.astype(BF16), pw_ref[gi]) * ps_ref[:, ls]
        mix = mix + _dot(ob.astype(BF16), wo_ref[GLA_VAL + gi * POOL_GDIM:GLA_VAL + (gi + 1) * POOL_GDIM, :])
    pbuf_ref[0:POOL_HALO, :] = p[tm - POOL_HALO:, :]
    o_ref[0] = h_ref[0] + mix


def _outproj0(h, oa, p, pool_w, pool_scale, w_out, *, tm=512):
    bsz, t, _ = h.shape
    return pl.pallas_call(
        functools.partial(_outproj0_body, tm=tm),
        grid=(bsz, t // tm),
        in_specs=[
            pl.BlockSpec((1, tm, D_MODEL), lambda b, i: (b, i, 0)),
            pl.BlockSpec((1, tm, GLA_VAL), lambda b, i: (b, i, 0)),
            pl.BlockSpec((1, tm, POOL_WIDTH), lambda b, i: (b, i, 0)),
            pl.BlockSpec((POOL_GROUPS, POOL_GDIM, POOL_GDIM), lambda b, i: (0, 0, 0)),
            pl.BlockSpec((1, POOL_WIDTH), lambda b, i: (0, 0)),
            pl.BlockSpec((D_MODEL, D_MODEL), lambda b, i: (0, 0)),
        ],
        out_specs=pl.BlockSpec((1, tm, D_MODEL), lambda b, i: (b, i, 0)),
        out_shape=jax.ShapeDtypeStruct((bsz, t, D_MODEL), F32),
        scratch_shapes=[pltpu.VMEM((POOL_HALO + tm, POOL_WIDTH), F32)],
        compiler_params=_cparams(("parallel", "arbitrary")),
        name="outproj0",
    )(h, oa, p, pool_w, pool_scale, w_out)


KEY_W = 2 * LANE
VAL_W = 2 * LANE
W1_Q, W1_KC, W1_VC, W1_KS, W1_VS, W1_KW, W1_VW, W1_GT = 0, 1024, 1152, 1280, 1536, 1792, 2048, 2304
W1_COLS = 2432


def _inproj1_body(h_ref, g_ref, w_ref, kf_ref, q_ref, kc_ref, vc_ref, ks_ref, vs_ref, kw_ref, vw_ref, gt_ref):
    xn = _rmsnorm(h_ref[...], g_ref[...]).astype(BF16)
    tm = xn.shape[0]
    q_ref[...] = (_dot(xn, w_ref[:, W1_Q:W1_Q + 1024]) * (NSA_HDIM ** -0.5)).astype(BF16)
    kc_ref[...] = _dot(xn, w_ref[:, W1_KC:W1_KC + NSA_KV])
    vc_ref[...] = _dot(xn, w_ref[:, W1_VC:W1_VC + NSA_KV])
    feat = kf_ref[:, 0:LANE].astype(F32)
    onehot = kf_ref[:, LANE:2 * LANE]
    ones = jnp.ones((tm, LANE), BF16)
    for g in range(NSA_KV_GROUPS):
        gl = slice(g * LANE, (g + 1) * LANE)
        ks = (_dot(xn, w_ref[:, W1_KS + g * LANE:W1_KS + (g + 1) * LANE]) + feat).astype(BF16)
        ks_ref[:, g * KEY_W:g * KEY_W + LANE] = ks
        ks_ref[:, g * KEY_W + LANE:(g + 1) * KEY_W] = onehot
        kw_ref[:, gl] = (_dot(xn, w_ref[:, W1_KW + g * LANE:W1_KW + (g + 1) * LANE]) + feat).astype(BF16)
        vs_ref[:, g * VAL_W:g * VAL_W + LANE] = _dot(xn, w_ref[:, W1_VS + g * LANE:W1_VS + (g + 1) * LANE]).astype(BF16)
        vs_ref[:, g * VAL_W + LANE:(g + 1) * VAL_W] = ones
        vw_ref[:, g * VAL_W:g * VAL_W + LANE] = _dot(xn, w_ref[:, W1_VW + g * LANE:W1_VW + (g + 1) * LANE]).astype(BF16)
        vw_ref[:, g * VAL_W + LANE:(g + 1) * VAL_W] = ones
    gt_ref[...] = jax.nn.sigmoid(_dot(xn, w_ref[:, W1_GT:W1_GT + LANE]))


def _inproj1(h, gain, w1, kfeat, *, t, tm=512):
    m = h.shape[0]
    per_seq = t // tm
    spec = lambda n: pl.BlockSpec((tm, n), lambda i: (i, 0))
    shp = lambda n, dt: jax.ShapeDtypeStruct((m, n), dt)
    g = NSA_KV_GROUPS
    return pl.pallas_call(
        _inproj1_body,
        grid=(m // tm,),
        in_specs=[
            pl.BlockSpec((tm, D_MODEL), lambda i: (i, 0)),
            pl.BlockSpec((1, D_MODEL), lambda i: (0, 0)),
            pl.BlockSpec((D_MODEL, W1_COLS), lambda i: (0, 0)),
            pl.BlockSpec((tm, KEY_W), lambda i: (i % per_seq, 0)),
        ],
        out_specs=[spec(1024), spec(NSA_KV), spec(NSA_KV), spec(g * KEY_W), spec(g * VAL_W), spec(g * LANE),
                   spec(g * VAL_W), spec(LANE)],
        out_shape=[shp(1024, BF16), shp(NSA_KV, F32), shp(NSA_KV, F32), shp(g * KEY_W, BF16), shp(g * VAL_W, BF16),
                   shp(g * LANE, BF16), shp(g * VAL_W, BF16), shp(LANE, F32)],
        compiler_params=_cparams(("parallel",)),
        name="inproj1",
    )(h, gain, w1, kfeat)


def _compress_body(z_ref, pea_ref, peb_ref, wa_ref, wb_ref, w2_ref, cf_ref, o_ref, sh_ref, *, nrow):
    z = z_ref[0]
    a = _dot((z + pea_ref[...]).astype(BF16), wa_ref[...])
    bm = _dot((z + peb_ref[...]).astype(BF16), wb_ref[...])
    sh_ref[0:nrow, :] = bm
    sh_ref[nrow:nrow + 8, :] = jnp.zeros((8, sh_ref.shape[1]), F32)
    x = a + sh_ref[1:nrow + 1, :]
    cdf = 0.5 * (1.0 + jnp.tanh(math.sqrt(2.0 / math.pi) * (x + 0.044715 * (x * x * x))))
    hid = (x * cdf).astype(BF16)
    for g in range(NSA_KV_GROUPS):
        o_ref[0, g] = (_dot(hid[:, g * CMP_HIDDEN:(g + 1) * CMP_HIDDEN], w2_ref[...]) + cf_ref[...]).astype(BF16)


def _compress(z, pea, peb, wa, wb, w2, cfeat):
    bsz, nrow, zw = z.shape
    hw = NSA_KV_GROUPS * CMP_HIDDEN
    return pl.pallas_call(
        functools.partial(_compress_body, nrow=nrow),
        grid=(bsz,),
        in_specs=[
            pl.BlockSpec((1, nrow, zw), lambda b: (b, 0, 0)),
            pl.BlockSpec((1, zw), lambda b: (0, 0)),
            pl.BlockSpec((1, zw), lambda b: (0, 0)),
            pl.BlockSpec((zw, hw), lambda b: (0, 0)),
            pl.BlockSpec((zw, hw), lambda b: (0, 0)),
            pl.BlockSpec((CMP_HIDDEN, LANE), lambda b: (0, 0)),
            pl.BlockSpec((nrow, LANE), lambda b: (0, 0)),
        ],
        out_specs=pl.BlockSpec((1, NSA_KV_GROUPS, nrow, LANE), lambda b: (b, 0, 0, 0)),
        out_shape=jax.ShapeDtypeStruct((bsz, NSA_KV_GROUPS, nrow, LANE), BF16),
        scratch_shapes=[pltpu.VMEM((nrow + 8, hw), F32)],
        compiler_params=_cparams(("parallel",)),
        name="compress",
    )(z, pea, peb, wa, wb, w2, cfeat)


def _fill_query_rows(qa_ref, q_ref, qf_ref, u=0):
    lane_lo = lax.broadcasted_iota(jnp.int32, (Q_BLOCK, LANE), 1) < NSA_HDIM
    for pr in range(NSA_PAIRS):
        even = q_ref[0, u * Q_BLOCK:(u + 1) * Q_BLOCK, pr * LANE:(pr + 1) * LANE]
        odd = pltpu.roll(even.astype(F32), NSA_HDIM, axis=1).astype(BF16)
        for hh, src in enumerate((even, odd)):
            h = 2 * pr + hh
            feat = jnp.broadcast_to(qf_ref[0, u, h:h + 1, :], (Q_BLOCK, LANE)).astype(BF16)
            qa_ref[h * Q_BLOCK:(h + 1) * Q_BLOCK, 0:LANE] = jnp.where(lane_lo, src, feat)


def _merge_pairs(o_ref, o, u=0):
    lane_lo = lax.broadcasted_iota(jnp.int32, (Q_BLOCK, LANE), 1) < NSA_HDIM
    for pr in range(NSA_PAIRS):
        a = o[(2 * pr) * Q_BLOCK:(2 * pr + 1) * Q_BLOCK]
        b = o[(2 * pr + 1) * Q_BLOCK:(2 * pr + 2) * Q_BLOCK]
        o_ref[0, u * Q_BLOCK:(u + 1) * Q_BLOCK, pr * LANE:(pr + 1) * LANE] = jnp.where(lane_lo, a, b).astype(o_ref.dtype)


def _head_tile(bias):
    return jnp.concatenate([bias] * NSA_HPG, axis=0)


SEL_TILES = 4


def _compressed_branch(u, qf_ref, q_ref, kc_ref, vc_ref, mcst_ref, oc_ref, qa_ref):
    t0 = (pl.program_id(2) * SEL_TILES + u) * Q_BLOCK
    ncmp_pad = kc_ref.shape[2]
    nq = Q_BLOCK
    _fill_query_rows(qa_ref, q_ref, qf_ref, u)
    tpos = t0 + lax.broadcasted_iota(jnp.int32, (nq, 1), 0)
    cmp_end = lax.broadcasted_iota(jnp.int32, (1, ncmp_pad), 1) * CMP_STRIDE + (CMP_BLOCK - 1)
    bias_c = jnp.where(tpos >= cmp_end, 0.0, NEG)
    any_c = jnp.where(tpos >= (CMP_BLOCK - 1), 1.0, 0.0)
    s = _dot_nt(qa_ref[...], kc_ref[0, 0]) + _head_tile(bias_c)
    e = jnp.exp(s - jnp.max(s, axis=-1, keepdims=True))
    p = e * (_head_tile(any_c) / jnp.sum(e, axis=-1, keepdims=True))
    _merge_pairs(oc_ref, _dot(p.astype(BF16), vc_ref[0, 0]), u)
    p_sum = p[0:nq]
    for h in range(1, NSA_HPG):
        p_sum = p_sum + p[h * nq:(h + 1) * nq]

    return sum(_dot_nt(mcst_ref[...], term) for term in _split3(p_sum))


def _nsa_select_body(qf_ref, q_ref, kc_ref, vc_ref, mcst_ref, oc_ref, sb_ref, lst_ref, *qa_refs):
    nt = SEL_TILES
    nq = Q_BLOCK
    wide = nt * nq
    imp = jnp.concatenate([_compressed_branch(u, qf_ref, q_ref, kc_ref, vc_ref, mcst_ref, oc_ref, qa_refs[u])
                           for u in range(nt)], axis=1)

    blk = lax.broadcasted_iota(jnp.int32, (MAX_SLC, wide), 0)
    tpos_l = pl.program_id(2) * wide + lax.broadcasted_iota(jnp.int32, (MAX_SLC, wide), 1)
    cur = tpos_l >> SLC_SHIFT
    forced = (blk == 0) | (blk == cur) | (blk == cur - 1)
    valid_b = blk * SLC_BLOCK <= tpos_l
    score = jnp.where(valid_b, jnp.where(forced, FORCE, imp), -1.0)
    blk_f = blk.astype(F32)
    sel = jnp.zeros((MAX_SLC, wide), F32)
    for _ in range(SLC_TOP):
        mx = jnp.max(score, axis=0, keepdims=True)
        first = jnp.min(jnp.where(score == mx, blk_f, float(MAX_SLC)), axis=0, keepdims=True)
        pick = blk_f == first
        sel = jnp.where(pick, 1.0, sel)
        score = jnp.where(pick, -2.0, score)
    own = (tpos_l >> (SLC_SHIFT + 1)) << 1
    sel = jnp.where(valid_b & (blk < own), sel, 0.0)
    for u in range(nt):
        sel_u = sel[:, u * nq:(u + 1) * nq]
        sb_ref[0, 0, u * nq:(u + 1) * nq, :] = jnp.where(sel_u.T > 0.5, 0.0, NEG).astype(BF16)

    used_b = jnp.concatenate(
        [jnp.broadcast_to(jnp.max(sel[:, u * nq:(u + 1) * nq], axis=1, keepdims=True), (MAX_SLC, LANE))
         for u in range(nt)], axis=1).astype(BF16)
    r = lax.broadcasted_iota(jnp.int32, (MAX_SLC, LANE), 0)
    c = lax.broadcasted_iota(jnp.int32, (MAX_SLC, LANE), 1)
    before = jnp.where(c < r, 1.0, 0.0).astype(BF16)
    pos = _dot(before, used_b)
    slot_w = (lax.broadcasted_iota(jnp.int32, (MAX_SLC, nt * LANE), 1) & (LANE - 1)).astype(F32)
    place = jnp.where((pos == slot_w) & (used_b > 0.5), 1.0, 0.0).astype(BF16)
    ids = lax.broadcasted_iota(jnp.int32, (8, LANE), 1).astype(F32).astype(BF16)
    lst = _dot(ids, place)
    cnt = _dot(jnp.ones((8, LANE), BF16), used_b)
    lane8 = lax.broadcasted_iota(jnp.int32, (8, nt * LANE), 1)
    filler = ((pl.program_id(2) * nt + (lane8 >> 7)) << 1).astype(F32)
    lst = jnp.where((lane8 & (LANE - 1)).astype(F32) < cnt, lst, filler)
    row0 = lax.broadcasted_iota(jnp.int32, (8, nt * LANE), 0) == 0
    out = jnp.where(row0, lst, cnt).astype(jnp.int32)
    for u in range(nt):
        lst_ref[0, 0, u] = out[:, u * LANE:(u + 1) * LANE]


def _nsa_select(qfeat, q, kcmp, vcmp, mcst):
    bsz, t, _ = q.shape
    ncmp_pad = kcmp.shape[2]
    nqt = t // Q_BLOCK
    gw = NSA_HPG * NSA_HDIM
    assert nqt % SEL_TILES == 0
    nt = SEL_TILES
    qspec = pl.BlockSpec((1, nt * Q_BLOCK, gw), lambda b, g, i: (b, i, g))
    cspec = pl.BlockSpec((1, 1, ncmp_pad, LANE), lambda b, g, i: (b, g, 0, 0))
    return pl.pallas_call(
        _nsa_select_body,
        grid=(bsz, NSA_KV_GROUPS, nqt // nt),
        in_specs=[
            pl.BlockSpec((1, nt, NSA_HPG, LANE), lambda b, g, i: (g, i, 0, 0)),
            qspec, cspec, cspec,
            pl.BlockSpec((MAX_SLC, ncmp_pad), lambda b, g, i: (0, 0)),
        ],
        out_specs=[qspec,
                   pl.BlockSpec((1, 1, nt * Q_BLOCK, MAX_SLC), lambda b, g, i: (b, g, i, 0)),
                   pl.BlockSpec((1, 1, nt, 8, LANE), lambda b, g, i: (b, g, i, 0, 0))],
        out_shape=[jax.ShapeDtypeStruct((bsz, t, NSA_HEADS * NSA_HDIM), BF16),
                   jax.ShapeDtypeStruct((bsz, NSA_KV_GROUPS, t, MAX_SLC), BF16),
                   jax.ShapeDtypeStruct((bsz, NSA_KV_GROUPS, nqt, 8, LANE), jnp.int32)],
        scratch_shapes=[pltpu.VMEM((ROWS, LANE), BF16)] * nt,
        compiler_params=_cparams(("parallel", "parallel", "parallel")),
        name="nsa_select",
    )(qfeat, q, kcmp, vcmp, mcst)


def _nsa_attend_body(lst_ref, cnt_ref, qf_ref, q_ref, sb_ref, ks_ref, vs_ref, kw_ref, vw_ref, os_ref, ow_ref,
                     qa_ref, kst0_ref, kst1_ref, vst0_ref, vst1_ref, s0_ref, s1_ref, m_ref, acc_ref):
    b = pl.program_id(0)
    g = pl.program_id(1)
    qi = pl.program_id(2)
    nqt = pl.num_programs(2)
    t0 = pl.multiple_of(qi * Q_BLOCK, Q_BLOCK)
    nq = Q_BLOCK
    _fill_query_rows(qa_ref, q_ref, qf_ref)
    sb = sb_ref[0, 0]
    for h in range(NSA_HPG):
        qa_ref[h * nq:(h + 1) * nq, LANE:2 * LANE] = sb
    qa_lo = qa_ref[:, 0:LANE]

    tile = (b * NSA_KV_GROUPS + g) * nqt + qi
    base = tile * LANE
    count = cnt_ref[tile]

    bufs = ((kst0_ref, vst0_ref, s0_ref), (kst1_ref, vst1_ref, s1_ref))

    def stage_and_score(c, slot):
        kst_ref, vst_ref, s_ref = bufs[slot]
        for j in range(STAGE_BLOCKS):
            idx = jnp.minimum(c * STAGE_BLOCKS + j, LANE - 1)
            r0 = pl.multiple_of(lst_ref[base + idx] * SLC_BLOCK, SLC_BLOCK)
            kst_ref[j * SLC_BLOCK:(j + 1) * SLC_BLOCK, :] = ks_ref[0, pl.ds(r0, SLC_BLOCK), :]
            vst_ref[j * SLC_BLOCK:(j + 1) * SLC_BLOCK, :] = vs_ref[0, pl.ds(r0, SLC_BLOCK), :]
        s_ref[...] = _dot_nt(qa_ref[...], kst_ref[...])

    def softmax_pv(slot):
        _, vst_ref, s_ref = bufs[slot]
        s = s_ref[...]
        m_old = m_ref[...]
        m_new = jnp.maximum(m_old, jnp.max(s, axis=-1, keepdims=True))
        acc_ref[...] = jnp.exp(m_old - m_new) * acc_ref[...] + _dot(jnp.exp(s - m_new).astype(BF16), vst_ref[...])
        m_ref[...] = m_new

    wk = WINDOW + nq
    w0 = pl.multiple_of(jnp.maximum(t0 - WINDOW, 0), nq)
    dist = (t0 + lax.broadcasted_iota(jnp.int32, (nq, 1), 0)) - (w0 + lax.broadcasted_iota(jnp.int32, (1, wk), 1))
    bias_w = jnp.where((dist >= 0) & (dist < WINDOW), 0.0, NEG)
    s = _dot_nt(qa_lo, kw_ref[0, pl.ds(w0, wk), :]) + _head_tile(bias_w)
    e = jnp.exp(s - jnp.max(s, axis=-1, keepdims=True))
    acc = _dot(e.astype(BF16), vw_ref[0, pl.ds(w0, wk), :])
    _merge_pairs(ow_ref, acc[:, 0:LANE] / acc[:, LANE:2 * LANE])

    ti = lax.broadcasted_iota(jnp.int32, (nq, nq), 0)
    tj = lax.broadcasted_iota(jnp.int32, (nq, nq), 1)
    s = _dot_nt(qa_lo, ks_ref[0, pl.ds(t0, nq), 0:LANE]) + _head_tile(jnp.where(tj <= ti, 0.0, NEG))
    m = jnp.max(s, axis=-1, keepdims=True)
    m_ref[...] = m
    acc_ref[...] = _dot(jnp.exp(s - m).astype(BF16), vs_ref[0, pl.ds(t0, nq), :])

    stage_and_score(0, 0)

    n_groups = (count + STAGE_BLOCKS - 1) // STAGE_BLOCKS

    def step(c, carry):
        stage_and_score(2 * c + 1, 1)
        softmax_pv(0)

        @pl.when(2 * c + 1 < n_groups)
        def _():
            stage_and_score(2 * c + 2, 0)
            softmax_pv(1)

        return carry

    lax.fori_loop(0, (n_groups + 1) // 2, step, 0)
    acc = acc_ref[...]
    _merge_pairs(os_ref, acc[:, 0:LANE] / acc[:, LANE:2 * LANE])


def _nsa_attend(lists, counts, qfeat, q, selbias, ks, vs, kw, vw):
    bsz, t, _ = q.shape
    nqt = t // Q_BLOCK
    gw = NSA_HPG * NSA_HDIM
    qspec = pl.BlockSpec((1, Q_BLOCK, gw), lambda b, g, i, lst, cnt: (b, i, g))
    out = jax.ShapeDtypeStruct((bsz, t, NSA_HEADS * NSA_HDIM), BF16)
    stage = STAGE_BLOCKS * SLC_BLOCK
    return pl.pallas_call(
        _nsa_attend_body,
        grid_spec=pltpu.PrefetchScalarGridSpec(
            num_scalar_prefetch=2,
            grid=(bsz, NSA_KV_GROUPS, nqt),
            in_specs=[
                pl.BlockSpec((1, 1, NSA_HPG, LANE), lambda b, g, i, lst, cnt: (g, i, 0, 0)),
                qspec,
                pl.BlockSpec((1, 1, Q_BLOCK, MAX_SLC), lambda b, g, i, lst, cnt: (b, g, i, 0)),
                pl.BlockSpec((1, t, KEY_W), lambda b, g, i, lst, cnt: (b, 0, g)),
                pl.BlockSpec((1, t, VAL_W), lambda b, g, i, lst, cnt: (b, 0, g)),
                pl.BlockSpec((1, t, LANE), lambda b, g, i, lst, cnt: (b, 0, g)),
                pl.BlockSpec((1, t, VAL_W), lambda b, g, i, lst, cnt: (b, 0, g)),
            ],
            out_specs=[qspec, qspec],
            scratch_shapes=[
                pltpu.VMEM((ROWS, KEY_W), BF16),
                pltpu.VMEM((stage, KEY_W), BF16),
                pltpu.VMEM((stage, KEY_W), BF16),
                pltpu.VMEM((stage, VAL_W), BF16),
                pltpu.VMEM((stage, VAL_W), BF16),
                pltpu.VMEM((ROWS, stage), F32),
                pltpu.VMEM((ROWS, stage), F32),
                pltpu.VMEM((ROWS, 1), F32),
                pltpu.VMEM((ROWS, VAL_W), F32),
            ],
        ),
        out_shape=[out, out],
        compiler_params=_cparams(("parallel", "parallel", "arbitrary")),
        name="nsa_attend",
    )(lists, counts, qfeat, q, selbias, ks, vs, kw, vw)


def _outproj1_body(h_ref, oc_ref, os_ref, ow_ref, gt_ref, ex_ref, wo_ref, o_ref):
    gt = gt_ref[...]
    hi = gt.astype(BF16)
    lo = (gt - hi.astype(F32)).astype(BF16)
    o = jnp.zeros(oc_ref.shape, F32)
    for c, br in enumerate((oc_ref, os_ref, ow_ref)):
        ge = _dot(hi, ex_ref[c]) + _dot(lo, ex_ref[c])
        o = o + ge * br[...].astype(F32)
    o_ref[...] = h_ref[...] + _dot(o.astype(BF16), wo_ref[...])


def _outproj1(h, oc, os_, ow, gates, expand, w_out, *, tm=512):
    m = h.shape[0]
    row = pl.BlockSpec((tm, D_MODEL), lambda i: (i, 0))
    return pl.pallas_call(
        _outproj1_body,
        grid=(m // tm,),
        in_specs=[row, row, row, row,
                  pl.BlockSpec((tm, LANE), lambda i: (i, 0)),
                  pl.BlockSpec((3, LANE, D_MODEL), lambda i: (0, 0, 0)),
                  pl.BlockSpec((D_MODEL, D_MODEL), lambda i: (0, 0))],
        out_specs=row,
        out_shape=jax.ShapeDtypeStruct((m, D_MODEL), F32),
        compiler_params=_cparams(("parallel",)),
        name="outproj1",
    )(h, oc, os_, ow, gates, expand, w_out)


def _pad_heads(w, heads, width, padded):
    lead = w.shape[:-1]
    w = w.reshape(lead + (heads, width))
    w = jnp.pad(w, [(0, 0)] * len(lead) + [(0, 0), (0, padded - width)])
    return w.reshape(lead + (heads * padded,))


def _dup_groups(w):
    lead = w.shape[:-1]
    w = w.reshape(lead + (NSA_KV_GROUPS, NSA_HDIM))
    return jnp.concatenate([w, w], axis=-1).reshape(lead + (NSA_KV_GROUPS * LANE,))


def _np_split3(x):
    out = []
    r = np.asarray(x, np.float64)
    for _ in range(3):
        part = np.asarray(np.asarray(r, np.float32).astype(jnp.bfloat16), np.float64)
        out.append(part.astype(np.float32))
        r = r - part
    return out


def _position_terms(pos):
    pos = np.asarray(pos)
    f = np.zeros((pos.shape[0], LANE), np.float32)
    f[:, FEAT0 + 0:FEAT0 + 3] = (pos >> SLC_SHIFT)[:, None]
    f[:, FEAT0 + 3:FEAT0 + 6] = (pos & (SLC_BLOCK - 1))[:, None]
    f[:, FEAT0 + 6:FEAT0 + 9] = 1.0
    return f


def _slope_terms(nqt):
    slopes = np.array([2.0 ** (-8.0 * (i + 1) / NSA_HEADS) for i in range(NSA_HEADS)], dtype=np.float32)
    slopes = slopes.reshape(NSA_HPG, NSA_KV_GROUPS).T.astype(np.float64)
    f = np.zeros((NSA_KV_GROUPS, nqt, NSA_HPG, LANE), np.float32)
    t0 = (np.arange(nqt) * Q_BLOCK).astype(np.float64)
    for k, part in enumerate(_np_split3(slopes * SLC_BLOCK)):
        f[:, :, :, FEAT0 + k] = part[:, None, :]
    for k, part in enumerate(_np_split3(slopes)):
        f[:, :, :, FEAT0 + 3 + k] = part[:, None, :]
    for k, part in enumerate(_np_split3(-slopes[:, None, :] * t0[None, :, None])):
        f[:, :, :, FEAT0 + 6 + k] = part
    return f


def _key_terms(t):
    pos = np.arange(t)
    onehot = np.zeros((t, MAX_SLC), np.float32)
    onehot[pos, pos >> SLC_SHIFT] = 1.0
    return np.concatenate([_position_terms(pos), onehot], axis=1)


def _cmp_to_slc_t(ncmp_pad):
    cs = np.arange(ncmp_pad) * CMP_STRIDE
    ss = np.arange(MAX_SLC) * SLC_BLOCK
    ov = np.minimum(cs[None] + CMP_BLOCK, ss[:, None] + SLC_BLOCK) - np.maximum(cs[None], ss[:, None])
    return (np.clip(ov, 0, None) / CMP_BLOCK).astype(np.float32)


def _gate_expand():
    e = np.zeros((3, LANE, NSA_HEADS * NSA_HDIM), np.float32)
    for c in range(3):
        for g in range(NSA_KV_GROUPS):
            for h in range(NSA_HPG):
                col = g * NSA_HPG * 3 + h * 3 + c
                base = (g * NSA_HPG + h) * NSA_HDIM
                e[c, col, base:base + NSA_HDIM] = 1.0
    return e


def _compress_weights(w1, w2, pe, dup):
    half = CMP_BLOCK // 2
    eye = jnp.eye(NSA_KV_GROUPS, dtype=w1.dtype)
    w1 = w1.reshape(2, half, NSA_HDIM, CMP_HIDDEN)
    wexp = jnp.einsum('sjdc,gh->sjgdhc', w1, eye).reshape(2, half * NSA_KV, NSA_KV_GROUPS * CMP_HIDDEN)
    pe = jnp.broadcast_to(pe.reshape(2, half, 1, NSA_HDIM), (2, half, NSA_KV_GROUPS, NSA_HDIM)).reshape(2, 1, half * NSA_KV)
    w2w = jnp.concatenate([w2, w2 if dup else jnp.zeros_like(w2)], axis=-1)
    return pe[0], pe[1], wexp[0].astype(BF16), wexp[1].astype(BF16), w2w.astype(BF16)


def _layer0_mixer(h3, norm_mix, a_w_in, a_gate_w2, a_gate_b, a_gla_norm, a_pool_w, a_pool_scale, a_w_out):
    bsz, t, d = h3.shape
    cuts = np.cumsum([GLA_KEY, GLA_KEY, GLA_VAL, GLA_VAL, GLA_GATE_RANK]).tolist()
    wq, wk, wv, wg, wgr, wp = jnp.split(a_w_in, cuts, axis=-1)
    w0 = jnp.concatenate([
        _pad_heads(wq, GLA_HEADS, GLA_DK, GLA_DK_PAD), _pad_heads(wk, GLA_HEADS, GLA_DK, GLA_DK_PAD),
        wv, wg, wp, jnp.pad(wgr, ((0, 0), (0, LANE - GLA_GATE_RANK)))], axis=-1).astype(BF16)
    w2 = jnp.pad(_pad_heads(a_gate_w2, GLA_HEADS, GLA_DK, GLA_DK_PAD), ((0, LANE - GLA_GATE_RANK), (0, 0))).astype(BF16)
    b2 = _pad_heads(a_gate_b, GLA_HEADS, GLA_DK, GLA_DK_PAD).reshape(1, -1)
    q, k, v, g, p, la = _inproj0(h3.reshape(bsz * t, d), norm_mix.reshape(1, d), w0, w2, b2)
    r3 = lambda z: z.reshape(bsz, t, -1)
    oa = _gla(r3(q), r3(k), r3(v), r3(g), r3(la), a_gla_norm.reshape(1, GLA_DV))
    return _outproj0(h3, oa, r3(p), a_pool_w.astype(BF16), a_pool_scale.reshape(1, -1), a_w_out.astype(BF16))


def _layer1_mixer(h3, norm_mix, c_w_in, c_cmp_pe, c_cmpk_w1, c_cmpk_w2, c_cmpv_w1, c_cmpv_w2, c_w_out):
    bsz, t, d = h3.shape
    assert t // SLC_BLOCK <= MAX_SLC and t >= WINDOW + Q_BLOCK
    nqt = t // Q_BLOCK
    cuts = (NSA_HEADS * NSA_HDIM + NSA_KV * np.arange(7)).tolist()
    wq, wkc, wvc, wks, wvs, wkw, wvw, wgt = jnp.split(c_w_in, cuts, axis=-1)
    w1 = jnp.concatenate([wq, wkc, wvc, _pad_heads(wks, NSA_KV_GROUPS, NSA_HDIM, LANE), _dup_groups(wvs),
                          _pad_heads(wkw, NSA_KV_GROUPS, NSA_HDIM, LANE), _dup_groups(wvw),
                          jnp.pad(wgt, ((0, 0), (0, LANE - wgt.shape[-1])))], axis=-1).astype(BF16)
    kfeat = jnp.asarray(_key_terms(t)).astype(BF16)
    q, kc, vc, ks, vs, kw, vw, gates = _inproj1(h3.reshape(bsz * t, d), norm_mix.reshape(1, d), w1, kfeat, t=t)
    nrow = t // CMP_STRIDE
    cfeat = jnp.asarray(_position_terms(np.arange(nrow) * CMP_STRIDE + CMP_BLOCK - 1))
    zero = jnp.zeros_like(cfeat)
    kcmp = _compress(kc.reshape(bsz, nrow, CMP_STRIDE * NSA_KV),
                     *_compress_weights(c_cmpk_w1, c_cmpk_w2, c_cmp_pe, dup=False), cfeat)
    vcmp = _compress(vc.reshape(bsz, nrow, CMP_STRIDE * NSA_KV),
                     *_compress_weights(c_cmpv_w1, c_cmpv_w2, c_cmp_pe, dup=True), zero)
    r3 = lambda z: z.reshape(bsz, t, -1)
    qfeat = jnp.asarray(_slope_terms(nqt))
    mcst = jnp.asarray(_cmp_to_slc_t(nrow)).astype(BF16)
    oc, selbias, lists = _nsa_select(qfeat, r3(q), kcmp, vcmp, mcst)
    os_, ow = _nsa_attend(lists[:, :, :, 0, :].reshape(-1), lists[:, :, :, 1, 0].reshape(-1), qfeat, r3(q), selbias,
                          r3(ks), r3(vs), r3(kw), r3(vw))
    f2 = lambda z: z.reshape(bsz * t, -1)
    out = _outproj1(h3.reshape(bsz * t, d), f2(oc), f2(os_), f2(ow), gates,
                    jnp.asarray(_gate_expand()).astype(BF16), c_w_out.astype(BF16))
    return out.reshape(bsz, t, d)


def kernel(x, norm_ffn1, ffn1_wg, ffn1_wu, ffn1_wd, norm_mix, norm_ffn2, ffn2_wg, ffn2_wu, ffn2_wd, a_w_in, a_gate_w2, a_gate_b, a_gla_norm, a_pool_w, a_pool_scale, a_w_out, c_w_in, c_cmp_pe, c_cmpk_w1, c_cmpk_w2, c_cmpv_w1, c_cmpv_w2, c_w_out, final_norm):
    bsz, t, d = x.shape
    depth = norm_ffn1.shape[0]
    fg = final_norm.reshape(1, d)
    h = x
    for l in range(depth):
        h = _ffn(h.reshape(bsz * t, d), norm_ffn1[l].reshape(1, d), ffn1_wg[l], ffn1_wu[l], ffn1_wd[l], fg,
                 final_norm=False).reshape(bsz, t, d)
        i = l // 2
        if l % 2 == 0:
            h = _layer0_mixer(h, norm_mix[l], a_w_in[i], a_gate_w2[i], a_gate_b[i], a_gla_norm[i],
                              a_pool_w[i], a_pool_scale[i], a_w_out[i])
        else:
            h = _layer1_mixer(h, norm_mix[l], c_w_in[i], c_cmp_pe[i], c_cmpk_w1[i], c_cmpk_w2[i],
                              c_cmpv_w1[i], c_cmpv_w2[i], c_w_out[i])
        h = _ffn(h.reshape(bsz * t, d), norm_ffn2[l].reshape(1, d), ffn2_wg[l], ffn2_wu[l], ffn2_wd[l], fg,
                 final_norm=(l == depth - 1)).reshape(bsz, t, d)
    return h
```
